```python
import math
import jax, jax.numpy as jnp
from jax import lax
import numpy as np


D_MODEL = 1024
BATCH = 2
SEQ = 8192
DEPTH = 4

N_MIXERS = 4
PLE_DIM = 256
HEAD_DIM = 64
N_HEADS = D_MODEL // HEAD_DIM
ROPE_THETA = 10000.0
NORM_EPS = 1e-6
Q_BLOCK = 128
LRU_WIDTH = D_MODEL
LRU_BLOCKS = N_HEADS
LRU_BLOCK_DIM = LRU_WIDTH // LRU_BLOCKS
CONV_WIDTH = 4
RGLRU_C = 8.0
DIL_PATTERNS = ((128, 1), (512, 4), (2048, 16))
DIFF_HEADS = D_MODEL // (2 * HEAD_DIM)
DIFF_SUBLN_EPS = 1e-5
FFN_HIDDEN = -(-(8 * D_MODEL) // (3 * 256)) * 256
N_A = (DEPTH + 3) // 4
N_B = (DEPTH + 2) // 4
N_C = (DEPTH + 1) // 4
N_D = DEPTH // 4

kernel_name = 'hybrid_interleaved_rglru_dilated_stickbreak_diffattn'


def rmsnorm(x, g, eps=NORM_EPS):
    xf = x.astype(jnp.float32)
    xf = xf * lax.rsqrt(jnp.mean(xf * xf, axis=-1, keepdims=True) + eps)
    return (xf * g.astype(jnp.float32)).astype(x.dtype)


def rope(t, positions):
    hd = t.shape[-1]
    inv = ROPE_THETA ** (-jnp.arange(0, hd, 2, dtype=jnp.float32) / hd)
    ang = positions.astype(jnp.float32)[..., None] * inv
    cos = jnp.cos(ang)[:, :, None, :]
    sin = jnp.sin(ang)[:, :, None, :]
    tf = t.astype(jnp.float32)
    t1, t2 = tf[..., :hd // 2], tf[..., hd // 2:]
    return jnp.concatenate([t1 * cos - t2 * sin, t2 * cos + t1 * sin], axis=-1).astype(t.dtype)


def to_blocks(t):
    B, S, H, e = t.shape
    return t.reshape(B, S // Q_BLOCK, Q_BLOCK, H, e).transpose(1, 0, 2, 3, 4)


def from_blocks(t):
    nb, B, blk, H, e = t.shape
    return t.transpose(1, 0, 2, 3, 4).reshape(B, nb * blk, H, e)


def _lin_rec_combine(c1, c2):
    a1, b1 = c1
    a2, b2 = c2
    return a1 * a2, a2 * b1 + b2


def rglru_block(h, w_in, conv_w, conv_b, gate_r_w, gate_r_b, gate_i_w, gate_i_b, a_param, w_out):
    B, S, _ = h.shape
    u = h @ w_in
    y = jax.nn.gelu(u[..., :LRU_WIDTH], approximate=True)
    xb = u[..., LRU_WIDTH:]
    xp = jnp.pad(xb, ((0, 0), (CONV_WIDTH - 1, 0), (0, 0)))
    xc = conv_b
    for tap in range(CONV_WIDTH):
        xc = xc + xp[:, tap:tap + S] * conv_w[tap]
    xh = xc.reshape(B, S, LRU_BLOCKS, LRU_BLOCK_DIM)
    r = jax.nn.sigmoid(jnp.einsum('bsnd,nde->bsne', xh, gate_r_w).reshape(B, S, LRU_WIDTH) + gate_r_b)
    ig = jax.nn.sigmoid(jnp.einsum('bsnd,nde->bsne', xh, gate_i_w).reshape(B, S, LRU_WIDTH) + gate_i_b)
    log_a = -RGLRU_C * r.astype(jnp.float32) * jax.nn.softplus(-a_param.astype(jnp.float32))
    a = jnp.exp(log_a)
    mult = jnp.sqrt(-jnp.expm1(2.0 * log_a))
    b = mult * (ig * xc).astype(jnp.float32)
    _, hs = lax.associative_scan(_lin_rec_combine, (a, b), axis=1)
    return (hs.astype(h.dtype) * y) @ w_out


def dilated_branch(q, k, v, window, dilation):
    B, S, H, hd = q.shape
    blk = window // dilation
    span = blk * dilation
    Sp = -(-S // span) * span
    pad = Sp - S
    nb = Sp // span

    def blocks(t):
        return jnp.pad(t, ((0, 0), (0, pad), (0, 0), (0, 0))).reshape(B, nb, blk, dilation, H, t.shape[-1])

    qb, kb, vb = blocks(q), blocks(k), blocks(v)
    kk = jnp.concatenate([jnp.concatenate([jnp.zeros_like(kb[:, :1]), kb[:, :-1]], axis=1), kb], axis=2)
    vv = jnp.concatenate([jnp.concatenate([jnp.zeros_like(vb[:, :1]), vb[:, :-1]], axis=1), vb], axis=2)
    i = np.arange(blk)[:, None]
    j = np.arange(2 * blk)[None, :]
    band = (j >= i) & (j <= i + blk)
    mask = band[None] & ((np.arange(nb)[:, None, None] > 0) | (j >= blk)[None])
    mask = jnp.asarray(mask)[None, :, :, None, None, :]
    s = jnp.einsum('bnqrhd,bnkrhd->bnqrhk', qb, kk).astype(jnp.float32)
    s = jnp.where(mask, s, -jnp.inf)
    m = jnp.max(s, axis=-1, keepdims=True)
    pe = jnp.exp(s - m)
    l = jnp.sum(pe, axis=-1, keepdims=True)
    o = jnp.einsum('bnqrhk,bnkrhd->bnqrhd', pe, vv.astype(jnp.float32))

    def unblock(t):
        return t.reshape(B, Sp, H, t.shape[-1])[:, :S]

    return unblock(o), unblock(m), unblock(l)


def dilated_attention(h, positions, w_qkv, w_out):
    B, S, _ = h.shape
    u = (h @ w_qkv).reshape(B, S, 3, N_HEADS, HEAD_DIM)
    q = rope(u[:, :, 0], positions) * (HEAD_DIM ** -0.5)
    k = rope(u[:, :, 1], positions)
    v = u[:, :, 2]
    outs = [dilated_branch(q, k, v, w, d) for (w, d) in DIL_PATTERNS]
    os_ = jnp.stack([o for (o, _, _) in outs])
    ms = jnp.stack([m for (_, m, _) in outs])
    ls = jnp.stack([l for (_, _, l) in outs])
    wgt = jnp.exp(ms - jnp.max(ms, axis=0, keepdims=True))
    out = jnp.sum(os_ * wgt, axis=0) / jnp.sum(ls * wgt, axis=0)
    return out.astype(h.dtype).reshape(B, S, D_MODEL) @ w_out


def stick_breaking_attention(h, w_qkv, w_out):
    B, S, _ = h.shape
    u = (h @ w_qkv).reshape(B, S, 3, N_HEADS, HEAD_DIM)
    q = u[:, :, 0] * (HEAD_DIM ** -0.5)
    k, v = u[:, :, 1], u[:, :, 2]
    kpos = jnp.arange(S)

    def one_block(args):
        qblk, start = args
        z = jnp.einsum('bqhd,bkhd->bhqk', qblk, k).astype(jnp.float32)
        qpos = start + jnp.arange(Q_BLOCK)
        mask = kpos[None, :] < qpos[:, None]
        lneg = jnp.where(mask, jax.nn.log_sigmoid(-z), 0.0)
        rest = lax.cumsum(lneg, axis=3, reverse=True) - lneg
        att = jnp.where(mask, jnp.exp(jax.nn.log_sigmoid(z) + rest), 0.0)
        return jnp.einsum('bhqk,bkhd->bqhd', att.astype(v.dtype), v)

    starts = jnp.arange(S // Q_BLOCK) * Q_BLOCK
    o = from_blocks(lax.map(one_block, (to_blocks(q), starts)))
    return o.reshape(B, S, D_MODEL) @ w_out


def differential_attention(h, positions, w_qkv, lq1, lk1, lq2, lk2, subln, w_out, layer_idx):
    B, S, _ = h.shape
    u = h @ w_qkv
    q = rope(u[..., :D_MODEL].reshape(B, S, 2 * DIFF_HEADS, HEAD_DIM), positions) * (HEAD_DIM ** -0.5)
    k = rope(u[..., D_MODEL:2 * D_MODEL].reshape(B, S, 2 * DIFF_HEADS, HEAD_DIM), positions)
    v = u[..., 2 * D_MODEL:].reshape(B, S, DIFF_HEADS, 2 * HEAD_DIM)
    lam_init = 0.8 - 0.6 * math.exp(-0.3 * layer_idx)
    lam = (jnp.exp(jnp.sum(lq1.astype(jnp.float32) * lk1.astype(jnp.float32)))
           - jnp.exp(jnp.sum(lq2.astype(jnp.float32) * lk2.astype(jnp.float32))) + lam_init)
    kpos = jnp.arange(S)

    def one_block(args):
        qblk, start = args
        s = jnp.einsum('bqhd,bkhd->bhqk', qblk, k).astype(jnp.float32)
        qpos = start + jnp.arange(Q_BLOCK)
        mask = kpos[None, :] <= qpos[:, None]
        a = jax.nn.softmax(jnp.where(mask, s, -jnp.inf), axis=-1)
        a = a.reshape(B, DIFF_HEADS, 2, Q_BLOCK, S)
        wmap = a[:, :, 0] - lam * a[:, :, 1]
        return jnp.einsum('bhqk,bkhe->bqhe', wmap.astype(v.dtype), v)

    starts = jnp.arange(S // Q_BLOCK) * Q_BLOCK
    o = from_blocks(lax.map(one_block, (to_blocks(q), starts)))
    o = rmsnorm(o, subln, eps=DIFF_SUBLN_EPS) * (1.0 - lam_init)
    return o.reshape(B, S, D_MODEL) @ w_out


def swiglu(h, w_in, w_out):
    u = h @ w_in
    return (jax.nn.silu(u[..., :FFN_HIDDEN]) * u[..., FFN_HIDDEN:]) @ w_out


def setup_inputs(seed: int = 0) -> dict:
    key = jax.random.key(seed)
    ks = iter(jax.random.split(key, 48))

    def dense(shape, fan_in):
        return jax.random.normal(next(ks), shape, jnp.float32) * (fan_in ** -0.5)

    def gain(shape):
        return 1.0 + 0.05 * jax.random.normal(next(ks), shape, jnp.float32)

    def bias(shape):
        return 0.01 * jax.random.normal(next(ks), shape, jnp.float32)

    x = jax.random.normal(next(ks), (BATCH, SEQ, D_MODEL), jnp.float32)
    p = jax.random.normal(next(ks), (DEPTH, BATCH, SEQ, PLE_DIM), jnp.float32)
    offsets = jax.random.randint(next(ks), (BATCH, 1), 0, 4096, dtype=jnp.int32)
    positions = (offsets + jnp.arange(SEQ, dtype=jnp.int32)[None, :]).astype(jnp.int32)

    ln_mix_pre = gain((DEPTH, D_MODEL))
    ln_mix_post = gain((DEPTH, D_MODEL))
    ln_ffn_pre = gain((DEPTH, D_MODEL))
    ln_ffn_post = gain((DEPTH, D_MODEL))
    ln_ple = gain((DEPTH, D_MODEL))
    w_ffn_in = dense((DEPTH, D_MODEL, 2 * FFN_HIDDEN), D_MODEL)
    w_ffn_out = dense((DEPTH, FFN_HIDDEN, D_MODEL), FFN_HIDDEN)
    w_ple_gate = dense((DEPTH, D_MODEL, D_MODEL), D_MODEL)
    b_ple_gate = bias((DEPTH, D_MODEL))
    w_ple_proj = dense((DEPTH, PLE_DIM, D_MODEL), PLE_DIM)

    a_w_in = dense((N_A, D_MODEL, 2 * LRU_WIDTH), D_MODEL)
    a_conv_w = dense((N_A, CONV_WIDTH, LRU_WIDTH), CONV_WIDTH)
    a_conv_b = bias((N_A, LRU_WIDTH))
    a_gate_r_w = dense((N_A, LRU_BLOCKS, LRU_BLOCK_DIM, LRU_BLOCK_DIM), LRU_BLOCK_DIM)
    a_gate_r_b = bias((N_A, LRU_WIDTH))
    a_gate_i_w = dense((N_A, LRU_BLOCKS, LRU_BLOCK_DIM, LRU_BLOCK_DIM), LRU_BLOCK_DIM)
    a_gate_i_b = bias((N_A, LRU_WIDTH))
    a_pow = jax.random.uniform(next(ks), (N_A, LRU_WIDTH), jnp.float32, minval=0.9, maxval=0.999)
    a_base = a_pow ** (1.0 / RGLRU_C)
    a_lambda = jnp.log(a_base) - jnp.log1p(-a_base)
    a_w_out = dense((N_A, LRU_WIDTH, D_MODEL), LRU_WIDTH)

    b_w_qkv = dense((N_B, D_MODEL, 3 * D_MODEL), D_MODEL)
    b_w_out = dense((N_B, D_MODEL, D_MODEL), D_MODEL)

    c_w_qkv = dense((N_C, D_MODEL, 3 * D_MODEL), D_MODEL)
    c_w_out = dense((N_C, D_MODEL, D_MODEL), D_MODEL)

    d_w_qkv = dense((N_D, D_MODEL, 3 * D_MODEL), D_MODEL)
    d_lambda_q1 = 0.1 * jax.random.normal(next(ks), (N_D, HEAD_DIM), jnp.float32)
    d_lambda_k1 = 0.1 * jax.random.normal(next(ks), (N_D, HEAD_DIM), jnp.float32)
    d_lambda_q2 = 0.1 * jax.random.normal(next(ks), (N_D, HEAD_DIM), jnp.float32)
    d_lambda_k2 = 0.1 * jax.random.normal(next(ks), (N_D, HEAD_DIM), jnp.float32)
    d_subln = gain((N_D, 2 * HEAD_DIM))
    d_w_out = dense((N_D, D_MODEL, D_MODEL), D_MODEL)

    return {'x': x, 'p': p, 'positions': positions,
            'ln_mix_pre': ln_mix_pre, 'ln_mix_post': ln_mix_post,
            'ln_ffn_pre': ln_ffn_pre, 'ln_ffn_post': ln_ffn_post, 'ln_ple': ln_ple,
            'w_ffn_in': w_ffn_in, 'w_ffn_out': w_ffn_out,
            'w_ple_gate': w_ple_gate, 'b_ple_gate': b_ple_gate, 'w_ple_proj': w_ple_proj,
            'a_w_in': a_w_in, 'a_conv_w': a_conv_w, 'a_conv_b': a_conv_b,
            'a_gate_r_w': a_gate_r_w, 'a_gate_r_b': a_gate_r_b,
            'a_gate_i_w': a_gate_i_w, 'a_gate_i_b': a_gate_i_b,
            'a_lambda': a_lambda, 'a_w_out': a_w_out,
            'b_w_qkv': b_w_qkv, 'b_w_out': b_w_out,
            'c_w_qkv': c_w_qkv, 'c_w_out': c_w_out,
            'd_w_qkv': d_w_qkv, 'd_lambda_q1': d_lambda_q1, 'd_lambda_k1': d_lambda_k1,
            'd_lambda_q2': d_lambda_q2, 'd_lambda_k2': d_lambda_k2,
            'd_subln': d_subln, 'd_w_out': d_w_out}


def reference(x, p, positions, ln_mix_pre, ln_mix_post, ln_ffn_pre, ln_ffn_post, ln_ple,
              w_ffn_in, w_ffn_out, w_ple_gate, b_ple_gate, w_ple_proj,
              a_w_in, a_conv_w, a_conv_b, a_gate_r_w, a_gate_r_b, a_gate_i_w, a_gate_i_b,
              a_lambda, a_w_out, b_w_qkv, b_w_out, c_w_qkv, c_w_out,
              d_w_qkv, d_lambda_q1, d_lambda_k1, d_lambda_q2, d_lambda_k2, d_subln, d_w_out):
    for i in range(DEPTH):
        kind = i % N_MIXERS
        j = i // N_MIXERS
        h = rmsnorm(x, ln_mix_pre[i])
        if kind == 0:
            y = rglru_block(h, a_w_in[j], a_conv_w[j], a_conv_b[j], a_gate_r_w[j], a_gate_r_b[j],
                            a_gate_i_w[j], a_gate_i_b[j], a_lambda[j], a_w_out[j])
        elif kind == 1:
            y = dilated_attention(h, positions, b_w_qkv[j], b_w_out[j])
        elif kind == 2:
            y = stick_breaking_attention(h, c_w_qkv[j], c_w_out[j])
        else:
            y = differential_attention(h, positions, d_w_qkv[j], d_lambda_q1[j], d_lambda_k1[j],
                                       d_lambda_q2[j], d_lambda_k2[j], d_subln[j], d_w_out[j], i)
        x = x + rmsnorm(y, ln_mix_post[i])
        h = rmsnorm(x, ln_ffn_pre[i])
        x = x + rmsnorm(swiglu(h, w_ffn_in[i], w_ffn_out[i]), ln_ffn_post[i])
        gate = jax.nn.sigmoid(x @ w_ple_gate[i] + b_ple_gate[i])
        x = x + rmsnorm(gate * (p[i] @ w_ple_proj[i]), ln_ple[i])
    return x
```

```python
import functools
import math

import numpy as np
import jax
import jax.numpy as jnp
from jax import lax
from jax.experimental import pallas as pl
from jax.experimental.pallas import tpu as pltpu

F32 = jnp.float32
BF16 = jnp.bfloat16

LANES = 128
HEAD_DIM = 64
ROPE_THETA = 10000.0
NORM_EPS = 1e-6
SUBLN_EPS = 1e-5
CONV_WIDTH = 4
RGLRU_C = 8.0
DIL_PATTERNS = ((128, 1), (512, 4), (2048, 16))
DIL_BLOCK = 128
DIL_REACH = max(w for w, _ in DIL_PATTERNS) // DIL_BLOCK
VMEM_LIMIT = 56 * 1024 * 1024
NEG_INF = float("-inf")


def _cparams(*sem):
    return pltpu.CompilerParams(dimension_semantics=sem, vmem_limit_bytes=VMEM_LIMIT)


def _rms(v, g, eps):
    return v * lax.rsqrt(jnp.mean(v * v, axis=-1, keepdims=True) + eps) * g


def _dot(a, b):
    return jnp.dot(a, b, preferred_element_type=F32)


def _dot_nt(a, b):
    return lax.dot_general(a, b, (((1,), (1,)), ((), ())), preferred_element_type=F32)


def _rope_tile(t, cos, sin_signed):
    lane = lax.broadcasted_iota(jnp.int32, t.shape, 1)
    first_half = (lane % HEAD_DIM) < (HEAD_DIM // 2)
    partner = jnp.where(first_half,
                        pltpu.roll(t, LANES - HEAD_DIM // 2, 1),
                        pltpu.roll(t, HEAD_DIM // 2, 1))
    return t * cos + partner * sin_signed


def _proj_kernel(x_ref, g_ref, w_ref, cos_ref, sin_ref, o_ref, h_scr, *, rope, qscale):
    j = pl.program_id(1)

    @pl.when(j == 0)
    def _():
        h_scr[...] = _rms(x_ref[...], g_ref[...], NORM_EPS).astype(BF16)

    acc = _dot(h_scr[...], w_ref[...])
    if qscale:
        acc = acc * jnp.where(j == 0, HEAD_DIM ** -0.5, 1.0).astype(F32)
    if not rope:
        o_ref[...] = acc.astype(o_ref.dtype)
        return

    @pl.when(j < 2)
    def _():
        cos = cos_ref[...]
        sin = sin_ref[...]
        for c in range(acc.shape[1] // LANES):
            sl = slice(c * LANES, (c + 1) * LANES)
            o_ref[:, sl] = _rope_tile(acc[:, sl], cos, sin).astype(o_ref.dtype)

    @pl.when(j >= 2)
    def _():
        o_ref[...] = acc.astype(o_ref.dtype)


def _norm_proj(x2, g, w, cos, sin, *, rope, qscale, out_dtype, tm=512, tn=1024):
    T, D = x2.shape
    N = w.shape[1]
    return pl.pallas_call(
        functools.partial(_proj_kernel, rope=rope, qscale=qscale),
        grid=(T // tm, N // tn),
        in_specs=[
            pl.BlockSpec((tm, D), lambda i, j: (i, 0)),
            pl.BlockSpec((1, D), lambda i, j: (0, 0)),
            pl.BlockSpec((D, tn), lambda i, j: (0, j)),
            pl.BlockSpec((tm, LANES), lambda i, j: (i, 0)),
            pl.BlockSpec((tm, LANES), lambda i, j: (i, 0)),
        ],
        out_specs=pl.BlockSpec((tm, tn), lambda i, j: (i, j)),
        out_shape=jax.ShapeDtypeStruct((T, N), out_dtype),
        scratch_shapes=[pltpu.VMEM((tm, D), BF16)],
        compiler_params=_cparams("parallel", "arbitrary"),
    )(x2, g, w, cos, sin)


def _outproj_kernel(a_ref, w_ref, x_ref, g_ref, o_ref):
    y = _dot(a_ref[...], w_ref[...])
    o_ref[...] = x_ref[...] + _rms(y, g_ref[...], NORM_EPS)


def _outproj(a2, w, x2, g, *, tm=512):
    T, D = x2.shape
    K = a2.shape[1]
    return pl.pallas_call(
        _outproj_kernel,
        grid=(T // tm,),
        in_specs=[
            pl.BlockSpec((tm, K), lambda i: (i, 0)),
            pl.BlockSpec((K, D), lambda i: (0, 0)),
            pl.BlockSpec((tm, D), lambda i: (i, 0)),
            pl.BlockSpec((1, D), lambda i: (0, 0)),
        ],
        out_specs=pl.BlockSpec((tm, D), lambda i: (i, 0)),
        out_shape=jax.ShapeDtypeStruct((T, D), F32),
        compiler_params=_cparams("parallel"),
    )(a2, w, x2, g)


def _ffn_ple_kernel(x_ref, gpre_ref, w1_ref, w2_ref, wo_ref, gpost_ref,
                    p_ref, wg_ref, bg_ref, wp_ref, gple_ref, o_ref, h_scr, acc_scr):
    f = pl.program_id(1)

    @pl.when(f == 0)
    def _():
        h_scr[...] = _rms(x_ref[...], gpre_ref[...], NORM_EPS).astype(BF16)
        acc_scr[...] = jnp.zeros_like(acc_scr)

    h = h_scr[...]
    u1 = _dot(h, w1_ref[...])
    u2 = _dot(h, w2_ref[...])
    act = (u1 * jax.nn.sigmoid(u1) * u2).astype(BF16)
    acc_scr[...] += _dot(act, wo_ref[...])

    @pl.when(f == pl.num_programs(1) - 1)
    def _():
        x1 = x_ref[...] + _rms(acc_scr[...], gpost_ref[...], NORM_EPS)
        gate = jax.nn.sigmoid(_dot(x1.astype(BF16), wg_ref[...]) + bg_ref[...])
        proj = _dot(p_ref[...].astype(BF16), wp_ref[...])
        o_ref[...] = x1 + _rms(gate * proj, gple_ref[...], NORM_EPS)


def _ffn_ple(x2, gpre, w_in, w_out, gpost, p2, wg, bg, wp, gple, *, tm=512, tf=256):
    T, D = x2.shape
    Fh = w_out.shape[0]
    nf = Fh // tf
    P = p2.shape[1]
    row = lambda i, f: (i, 0)
    const = lambda i, f: (0, 0)
    return pl.pallas_call(
        _ffn_ple_kernel,
        grid=(T // tm, nf),
        in_specs=[
            pl.BlockSpec((tm, D), row),
            pl.BlockSpec((1, D), const),
            pl.BlockSpec((D, tf), lambda i, f: (0, f)),
            pl.BlockSpec((D, tf), lambda i, f: (0, f + nf)),
            pl.BlockSpec((tf, D), lambda i, f: (f, 0)),
            pl.BlockSpec((1, D), const),
            pl.BlockSpec((tm, P), row),
            pl.BlockSpec((D, D), const),
            pl.BlockSpec((1, D), const),
            pl.BlockSpec((P, D), const),
            pl.BlockSpec((1, D), const),
        ],
        out_specs=pl.BlockSpec((tm, D), row),
        out_shape=jax.ShapeDtypeStruct((T, D), F32),
        scratch_shapes=[pltpu.VMEM((tm, D), BF16), pltpu.VMEM((tm, D), F32)],
        compiler_params=_cparams("parallel", "arbitrary"),
    )(x2, gpre, w_in, w_in, w_out, gpost, p2, wg, bg, wp, gple)


def _gelu_tanh(v):
    return 0.5 * v * (1.0 + jnp.tanh(math.sqrt(2.0 / math.pi) * (v + 0.044715 * v * v * v)))


def _rglru_kernel(y_ref, xb_ref, x_ref, cw_ref, cb_ref, wr_ref, br_ref, wi_ref, bi_ref,
                  lam_ref, wo_ref, gpost_ref, o_ref, ext_scr, h_scr, *, tc):
    t = pl.program_id(1)
    tail = 8
    width = xb_ref.shape[2]

    @pl.when(t == 0)
    def _():
        ext_scr[0:tail, :] = jnp.zeros((tail, width), F32)
        h_scr[...] = jnp.zeros_like(h_scr)

    ext_scr[tail:tail + tc, :] = xb_ref[0]
    xc = cb_ref[...] + ext_scr[tail:tail + tc, :] * cw_ref[CONV_WIDTH - 1:CONV_WIDTH, :]
    for back in range(1, CONV_WIDTH):
        tap = CONV_WIDTH - 1 - back
        xc = xc + ext_scr[tail - back:tail - back + tc, :] * cw_ref[tap:tap + 1, :]
    ext_scr[0:tail, :] = ext_scr[tc:tc + tail, :]

    r_parts, i_parts = [], []
    for c in range(width // LANES):
        xcb = xc[:, c * LANES:(c + 1) * LANES].astype(BF16)
        r_parts.append(_dot(xcb, wr_ref[c]))
        i_parts.append(_dot(xcb, wi_ref[c]))
    r = jax.nn.sigmoid(jnp.concatenate(r_parts, axis=1) + br_ref[...])
    ig = jax.nn.sigmoid(jnp.concatenate(i_parts, axis=1) + bi_ref[...])

    nl = -lam_ref[...]
    softplus_nl = jnp.maximum(nl, 0.0) + jnp.log1p(jnp.exp(-jnp.abs(nl)))
    log_a = (-RGLRU_C) * r * softplus_nl
    a = jnp.exp(log_a)
    b = jnp.sqrt(-jnp.tanh(log_a) * (a * a + 1.0)) * (ig * xc)

    rows = lax.broadcasted_iota(jnp.int32, (tc, width), 0)
    s = 1
    while s < tc:
        keep = rows >= s
        a_sh = jnp.where(keep, pltpu.roll(a, s, 0), 1.0)
        b_sh = jnp.where(keep, pltpu.roll(b, s, 0), 0.0)
        b = a * b_sh + b
        a = a * a_sh
        s *= 2
    hs = b + a * h_scr[...]
    h_scr[...] = hs[tc - 1:tc, :]

    z = (hs * _gelu_tanh(y_ref[0])).astype(BF16)
    o_ref[0] = x_ref[0] + _rms(_dot(z, wo_ref[...]), gpost_ref[...], NORM_EPS)


def _rglru_core(u3, x3, cw, cb, wr, br, wi, bi, lam, wo, gpost, *, tc=256):
    B, S, D = x3.shape
    W = u3.shape[2] // 2
    nl = W // LANES
    const2 = lambda b, t: (0, 0)
    return pl.pallas_call(
        functools.partial(_rglru_kernel, tc=tc),
        grid=(B, S // tc),
        in_specs=[
            pl.BlockSpec((1, tc, W), lambda b, t: (b, t, 0)),
            pl.BlockSpec((1, tc, W), lambda b, t: (b, t, 1)),
            pl.BlockSpec((1, tc, D), lambda b, t: (b, t, 0)),
            pl.BlockSpec((CONV_WIDTH, W), const2),
            pl.BlockSpec((1, W), const2),
            pl.BlockSpec((nl, LANES, LANES), lambda b, t: (0, 0, 0)),
            pl.BlockSpec((1, W), const2),
            pl.BlockSpec((nl, LANES, LANES), lambda b, t: (0, 0, 0)),
            pl.BlockSpec((1, W), const2),
            pl.BlockSpec((1, W), const2),
            pl.BlockSpec((W, D), const2),
            pl.BlockSpec((1, D), const2),
        ],
        out_specs=pl.BlockSpec((1, tc, D), lambda b, t: (b, t, 0)),
        out_shape=jax.ShapeDtypeStruct((B, S, D), F32),
        scratch_shapes=[pltpu.VMEM((tc + 8, W), F32), pltpu.VMEM((1, W), F32)],
        compiler_params=_cparams("parallel", "arbitrary"),
    )(u3, u3, x3, cw, cb, wr, br, wi, bi, lam, wo, gpost)


def _pair_block_diag(w):
    n, d, _ = w.shape
    w = w.reshape(n // 2, 2, d, d)
    z = jnp.zeros_like(w[:, 0])
    top = jnp.concatenate([w[:, 0], z], axis=2)
    bot = jnp.concatenate([z, w[:, 1]], axis=2)
    return jnp.concatenate([top, bot], axis=1)


def _split_pair(q):
    lane = lax.broadcasted_iota(jnp.int32, q.shape, 1)
    zero = jnp.zeros_like(q)
    return jnp.where(lane < HEAD_DIM, q, zero), jnp.where(lane >= HEAD_DIM, q, zero)


def _softmax_step(s, v, state):
    m, l, acc = state
    m_new = jnp.maximum(m, jnp.max(s, axis=-1, keepdims=True))
    alpha = jnp.exp(m - m_new)
    p = jnp.exp(s - m_new)
    l = alpha * l + jnp.sum(p, axis=-1, keepdims=True)
    acc = alpha * acc + _dot(p.astype(BF16), v)
    return m_new, l, acc


def _softmax_init(tq):
    return (jnp.full((tq, 1), NEG_INF, F32), jnp.zeros((tq, 1), F32), jnp.zeros((tq, LANES), F32))


def _dilated_kernel(q_ref, k_ref, v_ref, bias_ref, o_ref, *, blk):
    qi = pl.program_id(2)
    q0, q1 = _split_pair(q_ref[0])

    def step(o, state):
        start = pl.multiple_of((qi - o) * blk, blk)
        k = k_ref[0, pl.ds(start, blk), :]
        v = v_ref[0, pl.ds(start, blk), :]
        bias = bias_ref[o]
        st0 = _softmax_step(_dot_nt(q0, k) + bias, v, state[0])
        st1 = _softmax_step(_dot_nt(q1, k) + bias, v, state[1])
        return st0, st1

    state = step(0, (_softmax_init(blk), _softmax_init(blk)))
    state = lax.fori_loop(1, jnp.minimum(qi, DIL_REACH) + 1, step, state)
    (_, l0, a0), (_, l1, a1) = state
    lane = lax.broadcasted_iota(jnp.int32, (blk, LANES), 1)
    o_ref[0] = jnp.where(lane < HEAD_DIM, a0 / l0, a1 / l1).astype(o_ref.dtype)


def _dilated_bias():
    blk = DIL_BLOCK
    ti = np.arange(blk)[:, None]
    sj = np.arange(blk)[None, :]
    out = np.empty((DIL_REACH + 1, blk, blk), np.float32)
    for o in range(DIL_REACH + 1):
        delta = o * blk + ti - sj
        count = np.zeros((blk, blk), np.float64)
        for w, d in DIL_PATTERNS:
            count += (delta >= 0) & (delta <= w) & (delta % d == 0)
        with np.errstate(divide="ignore"):
            out[o] = np.log(count)
    return jnp.asarray(out)


def _dilated_attention(u3, bias):
    B, S, N3 = u3.shape
    D = N3 // 3
    npair = D // LANES
    blk = DIL_BLOCK
    return pl.pallas_call(
        functools.partial(_dilated_kernel, blk=blk),
        grid=(B, npair, S // blk),
        in_specs=[
            pl.BlockSpec((1, blk, LANES), lambda b, h, i: (b, i, h)),
            pl.BlockSpec((1, S, LANES), lambda b, h, i: (b, 0, npair + h)),
            pl.BlockSpec((1, S, LANES), lambda b, h, i: (b, 0, 2 * npair + h)),
            pl.BlockSpec((DIL_REACH + 1, blk, blk), lambda b, h, i: (0, 0, 0)),
        ],
        out_specs=pl.BlockSpec((1, blk, LANES), lambda b, h, i: (b, i, h)),
        out_shape=jax.ShapeDtypeStruct((B, S, D), BF16),
        compiler_params=_cparams("parallel", "parallel", "arbitrary"),
    )(u3, u3, u3, bias)


def _diff_kernel(q_ref, k_ref, v_ref, lq1_ref, lk1_ref, lq2_ref, lk2_ref, g_ref, o_ref,
                 *, blk, lam_init):
    qi = pl.program_id(2)
    q0, q1 = _split_pair(q_ref[0])
    row = lax.broadcasted_iota(jnp.int32, (blk, blk), 0)
    col = lax.broadcasted_iota(jnp.int32, (blk, blk), 1)
    causal = col <= row

    def tile(j, state, masked):
        start = pl.multiple_of(j * blk, blk)
        k = k_ref[0, pl.ds(start, blk), :]
        v = v_ref[0, pl.ds(start, blk), :]
        s0 = _dot_nt(q0, k)
        s1 = _dot_nt(q1, k)
        if masked:
            s0 = jnp.where(causal, s0, NEG_INF)
            s1 = jnp.where(causal, s1, NEG_INF)
        return _softmax_step(s0, v, state[0]), _softmax_step(s1, v, state[1])

    state = tile(qi, (_softmax_init(blk), _softmax_init(blk)), True)
    state = lax.fori_loop(0, qi, lambda j, st: tile(j, st, False), state)
    (_, l0, a0), (_, l1, a1) = state

    lam = (jnp.exp(jnp.sum(lq1_ref[...] * lk1_ref[...], axis=-1, keepdims=True))
           - jnp.exp(jnp.sum(lq2_ref[...] * lk2_ref[...], axis=-1, keepdims=True)) + lam_init)
    o = a0 / l0 - lam * (a1 / l1)
    o_ref[0] = (_rms(o, g_ref[...], SUBLN_EPS) * (1.0 - lam_init)).astype(o_ref.dtype)


def _diff_attention(u3, lq1, lk1, lq2, lk2, subln, lam_init, *, blk=256):
    B, S, N3 = u3.shape
    D = N3 // 3
    npair = D // LANES
    vec = lambda n: pl.BlockSpec((1, n), lambda b, h, i: (0, 0))
    return pl.pallas_call(
        functools.partial(_diff_kernel, blk=blk, lam_init=lam_init),
        grid=(B, npair, S // blk),
        in_specs=[
            pl.BlockSpec((1, blk, LANES), lambda b, h, i: (b, i, h)),
            pl.BlockSpec((1, S, LANES), lambda b, h, i: (b, 0, npair + h)),
            pl.BlockSpec((1, S, LANES), lambda b, h, i: (b, 0, 2 * npair + h)),
            vec(HEAD_DIM), vec(HEAD_DIM), vec(HEAD_DIM), vec(HEAD_DIM), vec(LANES),
        ],
        out_specs=pl.BlockSpec((1, blk, LANES), lambda b, h, i: (b, i, h)),
        out_shape=jax.ShapeDtypeStruct((B, S, D), BF16),
        compiler_params=_cparams("parallel", "parallel", "arbitrary"),
    )(u3, u3, u3, lq1, lk1, lq2, lk2, subln)


def _stick_kernel(q_ref, k_ref, v_ref, o_ref, *, blk):
    qi = pl.program_id(2)
    q0, q1 = _split_pair(q_ref[0])
    row = lax.broadcasted_iota(jnp.int32, (blk, blk), 0)
    col = lax.broadcasted_iota(jnp.int32, (blk, blk), 1)
    strict = col < row
    later = jnp.where(col < row, 1.0, 0.0).astype(BF16)

    def head(qh, k, v, state, masked):
        run, acc = state
        z = _dot_nt(qh, k)
        sp = jnp.maximum(z, 0.0) + jnp.log(1.0 + jnp.exp(-jnp.abs(z)))
        if masked:
            sp = jnp.where(strict, sp, 0.0)
        hi = sp.astype(BF16)
        lo = (sp - hi.astype(F32)).astype(BF16)
        after = _dot(hi, later) + _dot(lo, later)
        att = jnp.exp(z - sp - after - run)
        if masked:
            att = jnp.where(strict, att, 0.0)
        acc = acc + _dot(att.astype(BF16), v)
        run = run + jnp.sum(sp, axis=-1, keepdims=True)
        return run, acc

    def tile(j, state, masked):
        start = pl.multiple_of(j * blk, blk)
        k = k_ref[0, pl.ds(start, blk), :]
        v = v_ref[0, pl.ds(start, blk), :]
        return head(q0, k, v, state[0], masked), head(q1, k, v, state[1], masked)

    init = (jnp.zeros((blk, 1), F32), jnp.zeros((blk, LANES), F32))
    state = tile(qi, (init, init), True)
    state = lax.fori_loop(0, qi, lambda jj, st: tile(qi - 1 - jj, st, False), state)
    (_, a0), (_, a1) = state
    lane = lax.broadcasted_iota(jnp.int32, (blk, LANES), 1)
    o_ref[0] = jnp.where(lane < HEAD_DIM, a0, a1).astype(o_ref.dtype)


def _stick_attention(u3, *, blk=256):
    B, S, N3 = u3.shape
    D = N3 // 3
    npair = D // LANES
    return pl.pallas_call(
        functools.partial(_stick_kernel, blk=blk),
        grid=(B, npair, S // blk),
        in_specs=[
            pl.BlockSpec((1, blk, LANES), lambda b, h, i: (b, i, h)),
            pl.BlockSpec((1, S, LANES), lambda b, h, i: (b, 0, npair + h)),
            pl.BlockSpec((1, S, LANES), lambda b, h, i: (b, 0, 2 * npair + h)),
        ],
        out_specs=pl.BlockSpec((1, blk, LANES), lambda b, h, i: (b, i, h)),
        out_shape=jax.ShapeDtypeStruct((B, S, D), BF16),
        compiler_params=_cparams("parallel", "parallel", "arbitrary"),
    )(u3, u3, u3)


def _rope_tables(positions):
    half = HEAD_DIM // 2
    inv = ROPE_THETA ** (-jnp.arange(0, HEAD_DIM, 2, dtype=F32) / HEAD_DIM)
    ang = positions.astype(F32).reshape(-1, 1) * inv
    cos = jnp.cos(ang)
    sin = jnp.sin(ang)
    reps = LANES // HEAD_DIM
    cos_t = jnp.tile(jnp.concatenate([cos, cos], axis=1), (1, reps))
    sin_t = jnp.tile(jnp.concatenate([-sin, sin], axis=1), (1, reps))
    return cos_t, sin_t


def kernel(x, p, positions, ln_mix_pre, ln_mix_post, ln_ffn_pre, ln_ffn_post, ln_ple,
           w_ffn_in, w_ffn_out, w_ple_gate, b_ple_gate, w_ple_proj,
           a_w_in, a_conv_w, a_conv_b, a_gate_r_w, a_gate_r_b, a_gate_i_w, a_gate_i_b,
           a_lambda, a_w_out, b_w_qkv, b_w_out, c_w_qkv, c_w_out,
           d_w_qkv, d_lambda_q1, d_lambda_k1, d_lambda_q2, d_lambda_k2, d_subln, d_w_out):
    B, S, D = x.shape
    depth = p.shape[0]
    T = B * S
    cos_t, sin_t = _rope_tables(positions)
    bias = _dilated_bias()
    bf = lambda w: w.astype(BF16)
    vec = lambda v: v.reshape(1, -1)

    x2 = x.reshape(T, D)
    for i in range(depth):
        kind, j = i % 4, i // 4
        gpre, gpost = vec(ln_mix_pre[i]), vec(ln_mix_post[i])
        if kind == 0:
            u = _norm_proj(x2, gpre, bf(a_w_in[j]), cos_t, sin_t,
                           rope=False, qscale=False, out_dtype=F32)
            x2 = _rglru_core(
                u.reshape(B, S, -1), x2.reshape(B, S, D), a_conv_w[j], vec(a_conv_b[j]),
                bf(_pair_block_diag(a_gate_r_w[j])), vec(a_gate_r_b[j]),
                bf(_pair_block_diag(a_gate_i_w[j])), vec(a_gate_i_b[j]),
                vec(a_lambda[j]), bf(a_w_out[j]), gpost).reshape(T, D)
        else:
            if kind == 1:
                u = _norm_proj(x2, gpre, bf(b_w_qkv[j]), cos_t, sin_t,
                               rope=True, qscale=True, out_dtype=BF16)
                o = _dilated_attention(u.reshape(B, S, -1), bias)
                w_o = b_w_out[j]
            elif kind == 2:
                u = _norm_proj(x2, gpre, bf(c_w_qkv[j]), cos_t, sin_t,
                               rope=False, qscale=True, out_dtype=BF16)
                o = _stick_attention(u.reshape(B, S, -1))
                w_o = c_w_out[j]
            else:
                u = _norm_proj(x2, gpre, bf(d_w_qkv[j]), cos_t, sin_t,
                               rope=True, qscale=True, out_dtype=BF16)
                lam_init = 0.8 - 0.6 * math.exp(-0.3 * i)
                o = _diff_attention(u.reshape(B, S, -1), vec(d_lambda_q1[j]), vec(d_lambda_k1[j]),
                                    vec(d_lambda_q2[j]), vec(d_lambda_k2[j]), vec(d_subln[j]),
                                    lam_init)
                w_o = d_w_out[j]
            x2 = _outproj(o.reshape(T, D), bf(w_o), x2, gpost)
        x2 = _ffn_ple(x2, vec(ln_ffn_pre[i]), bf(w_ffn_in[i]), bf(w_ffn_out[i]),
                      vec(ln_ffn_post[i]), p[i].reshape(T, -1), bf(w_ple_gate[i]),
                      vec(b_ple_gate[i]), bf(w_ple_proj[i]), vec(ln_ple[i]))
    return x2.reshape(B, S, D)
```

```python
import functools
import math

import numpy as np
import jax
import jax.numpy as jnp
from jax import lax
from jax.experimental import pallas as pl
from jax.experimental.pallas import tpu as pltpu

F32 = jnp.float32
BF16 = jnp.bfloat16

LANES = 128
HEAD_DIM = 64
ROPE_THETA = 10000.0
NORM_EPS = 1e-6
SUBLN_EPS = 1e-5
CONV_WIDTH = 4
RGLRU_C = 8.0
DIL_PATTERNS = ((128, 1), (512, 4), (2048, 16))
DIL_BLOCK = 128
VMEM_LIMIT = 56 * 1024 * 1024
NEG_INF = float("-inf")


def _cparams(*sem):
    return pltpu.CompilerParams(dimension_semantics=sem, vmem_limit_bytes=VMEM_LIMIT)


def _rms(v, g, eps):
    return v * lax.rsqrt(jnp.mean(v * v, axis=-1, keepdims=True) + eps) * g


def _dot(a, b):
    return jnp.dot(a, b, preferred_element_type=F32)


def _dot_nt(a, b):
    return lax.dot_general(a, b, (((1,), (1,)), ((), ())), preferred_element_type=F32)


def _rope_tile(t, cos, sin_signed):
    lane = lax.broadcasted_iota(jnp.int32, t.shape, 1)
    first_half = (lane % HEAD_DIM) < (HEAD_DIM // 2)
    partner = jnp.where(first_half,
                        pltpu.roll(t, LANES - HEAD_DIM // 2, 1),
                        pltpu.roll(t, HEAD_DIM // 2, 1))
    return t * cos + partner * sin_signed


def _proj_kernel(x_ref, g_ref, w_ref, cos_ref, sin_ref, o_ref, h_scr, *, rope, qscale):
    j = pl.program_id(1)

    @pl.when(j == 0)
    def _():
        h_scr[...] = _rms(x_ref[...], g_ref[...], NORM_EPS).astype(BF16)

    acc = _dot(h_scr[...], w_ref[...])
    if qscale:
        acc = acc * jnp.where(j == 0, HEAD_DIM ** -0.5, 1.0).astype(F32)
    if not rope:
        o_ref[...] = acc.astype(o_ref.dtype)
        return

    @pl.when(j < 2)
    def _():
        cos = cos_ref[...]
        sin = sin_ref[...]
        for c in range(acc.shape[1] // LANES):
            sl = slice(c * LANES, (c + 1) * LANES)
            o_ref[:, sl] = _rope_tile(acc[:, sl], cos, sin).astype(o_ref.dtype)

    @pl.when(j >= 2)
    def _():
        o_ref[...] = acc.astype(o_ref.dtype)


def _norm_proj(x2, g, w, cos, sin, *, rope, qscale, out_dtype, tm=512, tn=1024):
    T, D = x2.shape
    N = w.shape[1]
    return pl.pallas_call(
        functools.partial(_proj_kernel, rope=rope, qscale=qscale),
        grid=(T // tm, N // tn),
        in_specs=[
            pl.BlockSpec((tm, D), lambda i, j: (i, 0)),
            pl.BlockSpec((1, D), lambda i, j: (0, 0)),
            pl.BlockSpec((D, tn), lambda i, j: (0, j)),
            pl.BlockSpec((tm, LANES), lambda i, j: (i, 0)),
            pl.BlockSpec((tm, LANES), lambda i, j: (i, 0)),
        ],
        out_specs=pl.BlockSpec((tm, tn), lambda i, j: (i, j)),
        out_shape=jax.ShapeDtypeStruct((T, N), out_dtype),
        scratch_shapes=[pltpu.VMEM((tm, D), BF16)],
        compiler_params=_cparams("parallel", "arbitrary"),
    )(x2, g, w, cos, sin)


def _outproj_kernel(a_ref, w_ref, x_ref, g_ref, o_ref):
    y = _dot(a_ref[...], w_ref[...])
    o_ref[...] = x_ref[...] + _rms(y, g_ref[...], NORM_EPS)


def _outproj(a2, w, x2, g, *, tm=512):
    T, D = x2.shape
    K = a2.shape[1]
    return pl.pallas_call(
        _outproj_kernel,
        grid=(T // tm,),
        in_specs=[
            pl.BlockSpec((tm, K), lambda i: (i, 0)),
            pl.BlockSpec((K, D), lambda i: (0, 0)),
            pl.BlockSpec((tm, D), lambda i: (i, 0)),
            pl.BlockSpec((1, D), lambda i: (0, 0)),
        ],
        out_specs=pl.BlockSpec((tm, D), lambda i: (i, 0)),
        out_shape=jax.ShapeDtypeStruct((T, D), F32),
        compiler_params=_cparams("parallel"),
    )(a2, w, x2, g)


def _ffn_ple_kernel(x_ref, gpre_ref, w1_ref, w2_ref, wo_ref, gpost_ref,
                    p_ref, wg_ref, bg_ref, wp_ref, gple_ref, o_ref, h_scr, acc_scr):
    f = pl.program_id(1)

    @pl.when(f == 0)
    def _():
        h_scr[...] = _rms(x_ref[...], gpre_ref[...], NORM_EPS).astype(BF16)
        acc_scr[...] = jnp.zeros_like(acc_scr)

    h = h_scr[...]
    u1 = _dot(h, w1_ref[...])
    u2 = _dot(h, w2_ref[...])
    act = (u1 * jax.nn.sigmoid(u1) * u2).astype(BF16)
    acc_scr[...] += _dot(act, wo_ref[...])

    @pl.when(f == pl.num_programs(1) - 1)
    def _():
        x1 = x_ref[...] + _rms(acc_scr[...], gpost_ref[...], NORM_EPS)
        gate = jax.nn.sigmoid(_dot(x1.astype(BF16), wg_ref[...]) + bg_ref[...])
        proj = _dot(p_ref[...].astype(BF16), wp_ref[...])
        o_ref[...] = x1 + _rms(gate * proj, gple_ref[...], NORM_EPS)


def _ffn_ple(x2, gpre, w_in, w_out, gpost, p2, wg, bg, wp, gple, *, tm=512, tf=256):
    T, D = x2.shape
    Fh = w_out.shape[0]
    nf = Fh // tf
    P = p2.shape[1]
    row = lambda i, f: (i, 0)
    const = lambda i, f: (0, 0)
    return pl.pallas_call(
        _ffn_ple_kernel,
        grid=(T // tm, nf),
        in_specs=[
            pl.BlockSpec((tm, D), row),
            pl.BlockSpec((1, D), const),
            pl.BlockSpec((D, tf), lambda i, f: (0, f)),
            pl.BlockSpec((D, tf), lambda i, f: (0, f + nf)),
            pl.BlockSpec((tf, D), lambda i, f: (f, 0)),
            pl.BlockSpec((1, D), const),
            pl.BlockSpec((tm, P), row),
            pl.BlockSpec((D, D), const),
            pl.BlockSpec((1, D), const),
            pl.BlockSpec((P, D), const),
            pl.BlockSpec((1, D), const),
        ],
        out_specs=pl.BlockSpec((tm, D), row),
        out_shape=jax.ShapeDtypeStruct((T, D), F32),
        scratch_shapes=[pltpu.VMEM((tm, D), BF16), pltpu.VMEM((tm, D), F32)],
        compiler_params=_cparams("parallel", "arbitrary"),
    )(x2, gpre, w_in, w_in, w_out, gpost, p2, wg, bg, wp, gple)


def _gelu_tanh(v):
    return 0.5 * v * (1.0 + jnp.tanh(math.sqrt(2.0 / math.pi) * (v + 0.044715 * v * v * v)))


def _rglru_kernel(y_ref, xb_ref, x_ref, cw_ref, cb_ref, wr_ref, br_ref, wi_ref, bi_ref,
                  lam_ref, wo_ref, gpost_ref, o_ref, ext_scr, h_scr, *, tc):
    t = pl.program_id(1)
    tail = 8
    width = xb_ref.shape[2]

    @pl.when(t == 0)
    def _():
        ext_scr[0:tail, :] = jnp.zeros((tail, width), F32)
        h_scr[...] = jnp.zeros_like(h_scr)

    ext_scr[tail:tail + tc, :] = xb_ref[0]
    xc = cb_ref[...] + ext_scr[tail:tail + tc, :] * cw_ref[CONV_WIDTH - 1:CONV_WIDTH, :]
    for back in range(1, CONV_WIDTH):
        tap = CONV_WIDTH - 1 - back
        xc = xc + ext_scr[tail - back:tail - back + tc, :] * cw_ref[tap:tap + 1, :]
    ext_scr[0:tail, :] = ext_scr[tc:tc + tail, :]

    r_parts, i_parts = [], []
    for c in range(width // LANES):
        xcb = xc[:, c * LANES:(c + 1) * LANES].astype(BF16)
        r_parts.append(_dot(xcb, wr_ref[c]))
        i_parts.append(_dot(xcb, wi_ref[c]))
    r = jax.nn.sigmoid(jnp.concatenate(r_parts, axis=1) + br_ref[...])
    ig = jax.nn.sigmoid(jnp.concatenate(i_parts, axis=1) + bi_ref[...])

    nl = -lam_ref[...]
    softplus_nl = jnp.maximum(nl, 0.0) + jnp.log1p(jnp.exp(-jnp.abs(nl)))
    log_a = (-RGLRU_C) * r * softplus_nl
    a = jnp.exp(log_a)
    b = jnp.sqrt(-jnp.tanh(log_a) * (a * a + 1.0)) * (ig * xc)

    rows = lax.broadcasted_iota(jnp.int32, (tc, width), 0)
    s = 1
    while s < tc:
        keep = rows >= s
        a_sh = jnp.where(keep, pltpu.roll(a, s, 0), 1.0)
        b_sh = jnp.where(keep, pltpu.roll(b, s, 0), 0.0)
        b = a * b_sh + b
        a = a * a_sh
        s *= 2
    hs = b + a * h_scr[...]
    h_scr[...] = hs[tc - 1:tc, :]

    z = (hs * _gelu_tanh(y_ref[0])).astype(BF16)
    o_ref[0] = x_ref[0] + _rms(_dot(z, wo_ref[...]), gpost_ref[...], NORM_EPS)


def _rglru_core(u3, x3, cw, cb, wr, br, wi, bi, lam, wo, gpost, *, tc=256):
    B, S, D = x3.shape
    W = u3.shape[2] // 2
    nl = W // LANES
    const2 = lambda b, t: (0, 0)
    return pl.pallas_call(
        functools.partial(_rglru_kernel, tc=tc),
        grid=(B, S // tc),
        in_specs=[
            pl.BlockSpec((1, tc, W), lambda b, t: (b, t, 0)),
            pl.BlockSpec((1, tc, W), lambda b, t: (b, t, 1)),
            pl.BlockSpec((1, tc, D), lambda b, t: (b, t, 0)),
            pl.BlockSpec((CONV_WIDTH, W), const2),
            pl.BlockSpec((1, W), const2),
            pl.BlockSpec((nl, LANES, LANES), lambda b, t: (0, 0, 0)),
            pl.BlockSpec((1, W), const2),
            pl.BlockSpec((nl, LANES, LANES), lambda b, t: (0, 0, 0)),
            pl.BlockSpec((1, W), const2),
            pl.BlockSpec((1, W), const2),
            pl.BlockSpec((W, D), const2),
            pl.BlockSpec((1, D), const2),
        ],
        out_specs=pl.BlockSpec((1, tc, D), lambda b, t: (b, t, 0)),
        out_shape=jax.ShapeDtypeStruct((B, S, D), F32),
        scratch_shapes=[pltpu.VMEM((tc + 8, W), F32), pltpu.VMEM((1, W), F32)],
        compiler_params=_cparams("parallel", "arbitrary"),
    )(u3, u3, x3, cw, cb, wr, br, wi, bi, lam, wo, gpost)


def _pair_block_diag(w):
    n, d, _ = w.shape
    w = w.reshape(n // 2, 2, d, d)
    z = jnp.zeros_like(w[:, 0])
    top = jnp.concatenate([w[:, 0], z], axis=2)
    bot = jnp.concatenate([z, w[:, 1]], axis=2)
    return jnp.concatenate([top, bot], axis=1)


def _split_pair(q):
    lane = lax.broadcasted_iota(jnp.int32, q.shape, 1)
    zero = jnp.zeros_like(q)
    return jnp.where(lane < HEAD_DIM, q, zero), jnp.where(lane >= HEAD_DIM, q, zero)


def _dilated_kernel(q_ref, kp_ref, kc_ref, vp_ref, vc_ref, band_ref, o_ref, lse_ref,
                    s_scr, p_scr, *, blk):
    n = pl.program_id(1)
    band = band_ref[...]
    bias = jnp.concatenate(
        [band[:, :blk] + jnp.where(n == 0, NEG_INF, 0.0).astype(F32), band[:, blk:]], axis=1)
    lane = lax.broadcasted_iota(jnp.int32, (blk, LANES), 1)
    low = lane < HEAD_DIM
    npair = q_ref.shape[2] // LANES
    pair = lambda pr: slice(pr * LANES, (pr + 1) * LANES)

    for pr in range(npair):
        kcat = jnp.concatenate([kp_ref[0, :, pair(pr)], kc_ref[0, :, pair(pr)]], axis=0)
        for c, qh in enumerate(_split_pair(q_ref[0, :, pair(pr)])):
            s_scr[2 * pr + c] = _dot_nt(qh, kcat) + bias

    row_max = []
    for c in range(2 * npair):
        s = s_scr[c]
        m = jnp.max(s, axis=-1, keepdims=True)
        p_scr[c] = jnp.exp(s - m).astype(BF16)
        row_max.append(m)

    ones = jnp.ones((2 * blk, LANES), BF16)
    low2 = lax.broadcasted_iota(jnp.int32, (2 * blk, LANES), 1) < HEAD_DIM
    for pr in range(npair):
        vcat = jnp.concatenate([vp_ref[0, :, pair(pr)], vc_ref[0, :, pair(pr)]], axis=0)
        a0 = _dot(p_scr[2 * pr], jnp.where(low2, vcat, ones))
        a1 = _dot(p_scr[2 * pr + 1], jnp.where(low2, ones, vcat))
        num = jnp.where(low, a0, a1)
        den = pltpu.roll(jnp.where(low, a1, a0), HEAD_DIM, 1)
        m = jnp.where(low, row_max[2 * pr], row_max[2 * pr + 1])
        o_ref[0, :, pair(pr)] = (num / den).astype(o_ref.dtype)
        lse_ref[0, :, pair(pr)] = m + jnp.log(den)


def _dilated_band():
    blk = DIL_BLOCK
    i = np.arange(blk)[:, None]
    j = np.arange(2 * blk)[None, :]
    return jnp.asarray(np.where((j >= i) & (j <= i + blk), 0.0, -np.inf).astype(np.float32))


def _dilated_pattern(u3, band, dilation):
    B, S, N3 = u3.shape
    D = N3 // 3
    blk = DIL_BLOCK
    d = dilation
    uv = u3.reshape(B, S // d, d * N3)
    prev = lambda n: jnp.maximum(n - 1, 0)
    out_spec = pl.BlockSpec((1, blk, D), lambda b, n, r: (b, n, r))
    o, lse = pl.pallas_call(
        functools.partial(_dilated_kernel, blk=blk),
        grid=(B, S // (d * blk), d),
        in_specs=[
            pl.BlockSpec((1, blk, D), lambda b, n, r: (b, n, 3 * r)),
            pl.BlockSpec((1, blk, D), lambda b, n, r: (b, prev(n), 3 * r + 1)),
            pl.BlockSpec((1, blk, D), lambda b, n, r: (b, n, 3 * r + 1)),
            pl.BlockSpec((1, blk, D), lambda b, n, r: (b, prev(n), 3 * r + 2)),
            pl.BlockSpec((1, blk, D), lambda b, n, r: (b, n, 3 * r + 2)),
            pl.BlockSpec((blk, 2 * blk), lambda b, n, r: (0, 0)),
        ],
        out_specs=[out_spec, out_spec],
        out_shape=[jax.ShapeDtypeStruct((B, S // d, d * D), BF16),
                   jax.ShapeDtypeStruct((B, S // d, d * D), F32)],
        scratch_shapes=[pltpu.VMEM((D // HEAD_DIM, blk, 2 * blk), F32),
                        pltpu.VMEM((D // HEAD_DIM, blk, 2 * blk), BF16)],
        compiler_params=_cparams("parallel", "parallel", "parallel"),
    )(uv, uv, uv, uv, uv, band)
    return o.reshape(B * S, D), lse.reshape(B * S, D)


def _merge_outproj_kernel(o1_ref, o2_ref, o3_ref, l1_ref, l2_ref, l3_ref, w_ref, x_ref, g_ref, o_ref):
    l1, l2, l3 = l1_ref[...], l2_ref[...], l3_ref[...]
    m = jnp.maximum(jnp.maximum(l1, l2), l3)
    w1, w2, w3 = jnp.exp(l1 - m), jnp.exp(l2 - m), jnp.exp(l3 - m)
    num = w1 * o1_ref[...].astype(F32) + w2 * o2_ref[...].astype(F32) + w3 * o3_ref[...].astype(F32)
    a = (num / (w1 + w2 + w3)).astype(BF16)
    o_ref[...] = x_ref[...] + _rms(_dot(a, w_ref[...]), g_ref[...], NORM_EPS)


def _merge_outproj(os_, lses, w, x2, g, *, tm=512):
    T, D = x2.shape
    row = pl.BlockSpec((tm, D), lambda i: (i, 0))
    return pl.pallas_call(
        _merge_outproj_kernel,
        grid=(T // tm,),
        in_specs=[row] * 6 + [pl.BlockSpec((D, D), lambda i: (0, 0)), row,
                              pl.BlockSpec((1, D), lambda i: (0, 0))],
        out_specs=row,
        out_shape=jax.ShapeDtypeStruct((T, D), F32),
        compiler_params=_cparams("parallel"),
    )(*os_, *lses, w, x2, g)


def _diff_kernel(q_ref, k_ref, v_ref, lq1_ref, lk1_ref, lq2_ref, lk2_ref, g_ref, o_ref,
                 s_scr, p_scr, m_scr, alpha_scr, acc_scr, *, blk, lam_init):
    qi = pl.program_id(2)
    qs = _split_pair(q_ref[0])
    m_scr[...] = jnp.full(m_scr.shape, NEG_INF, F32)
    acc_scr[...] = jnp.zeros(acc_scr.shape, F32)
    lane_tiles = blk // LANES

    def tile(j, masked):
        start = pl.multiple_of(j * blk, blk)
        k = k_ref[0, pl.ds(start, blk), :]
        v1 = jnp.concatenate([v_ref[0, pl.ds(start, blk), :], jnp.ones((blk, LANES), BF16)], axis=1)
        for c in range(2):
            s = _dot_nt(qs[c], k)
            if masked:
                row = lax.broadcasted_iota(jnp.int32, (blk, blk), 0)
                col = lax.broadcasted_iota(jnp.int32, (blk, blk), 1)
                s = jnp.where(col <= row, s, NEG_INF)
            s_scr[c] = s
        for c in range(2):
            m_old = m_scr[c]
            row_max = jnp.max(s_scr[c], axis=-1, keepdims=True)
            m_new = jnp.maximum(m_old, jnp.broadcast_to(row_max, (blk, LANES)))
            alpha_scr[c] = jnp.exp(m_old - m_new)
            m_scr[c] = m_new
        for c in range(2):
            m = jnp.concatenate([m_scr[c]] * lane_tiles, axis=1)
            p_scr[c] = jnp.exp(s_scr[c] - m).astype(BF16)
        for c in range(2):
            alpha = jnp.concatenate([alpha_scr[c]] * 2, axis=1)
            acc_scr[c] = alpha * acc_scr[c] + _dot(p_scr[c], v1)

    tile(qi, True)

    def body(j, carry):
        tile(j, False)
        return carry

    lax.fori_loop(0, qi, body, 0)

    lam = (jnp.exp(jnp.sum(lq1_ref[...] * lk1_ref[...], axis=-1, keepdims=True))
           - jnp.exp(jnp.sum(lq2_ref[...] * lk2_ref[...], axis=-1, keepdims=True)) + lam_init)
    a0, a1 = acc_scr[0], acc_scr[1]
    o = a0[:, :LANES] / a0[:, LANES:] - lam * (a1[:, :LANES] / a1[:, LANES:])
    o_ref[0] = (_rms(o, g_ref[...], SUBLN_EPS) * (1.0 - lam_init)).astype(o_ref.dtype)


def _diff_attention(u3, lq1, lk1, lq2, lk2, subln, lam_init, *, blk=512):
    B, S, N3 = u3.shape
    D = N3 // 3
    npair = D // LANES
    vec = lambda n: pl.BlockSpec((1, n), lambda b, h, i: (0, 0))
    return pl.pallas_call(
        functools.partial(_diff_kernel, blk=blk, lam_init=lam_init),
        grid=(B, npair, S // blk),
        in_specs=[
            pl.BlockSpec((1, blk, LANES), lambda b, h, i: (b, i, h)),
            pl.BlockSpec((1, S, LANES), lambda b, h, i: (b, 0, npair + h)),
            pl.BlockSpec((1, S, LANES), lambda b, h, i: (b, 0, 2 * npair + h)),
            vec(HEAD_DIM), vec(HEAD_DIM), vec(HEAD_DIM), vec(HEAD_DIM), vec(LANES),
        ],
        out_specs=pl.BlockSpec((1, blk, LANES), lambda b, h, i: (b, i, h)),
        out_shape=jax.ShapeDtypeStruct((B, S, D), BF16),
        scratch_shapes=[pltpu.VMEM((2, blk, blk), F32), pltpu.VMEM((2, blk, blk), BF16),
                        pltpu.VMEM((2, blk, LANES), F32), pltpu.VMEM((2, blk, LANES), F32),
                        pltpu.VMEM((2, blk, 2 * LANES), F32)],
        compiler_params=_cparams("parallel", "parallel", "arbitrary"),
    )(u3, u3, u3, lq1, lk1, lq2, lk2, subln)


def _stick_kernel(q_ref, k_ref, v_ref, o_ref, z_scr, hi_scr, lo_scr, cs_scr, att_scr,
                  off_scr, run_scr, acc_scr, *, blk, sub):
    qi = pl.program_id(2)
    qs = _split_pair(q_ref[0])
    nsub = blk // sub
    r2 = lax.broadcasted_iota(jnp.int32, (sub, sub), 0)
    c2 = lax.broadcasted_iota(jnp.int32, (sub, sub), 1)
    later = jnp.where(c2 < r2, 1.0, 0.0).astype(BF16)
    run_scr[...] = jnp.zeros(run_scr.shape, F32)
    acc_scr[...] = jnp.zeros(acc_scr.shape, F32)

    def tile(j, masked):
        start = pl.multiple_of(j * blk, blk)
        k = k_ref[0, pl.ds(start, blk), :]
        v = v_ref[0, pl.ds(start, blk), :]
        if masked:
            row = lax.broadcasted_iota(jnp.int32, (blk, blk), 0)
            col = lax.broadcasted_iota(jnp.int32, (blk, blk), 1)
            strict = col < row
        for h in range(2):
            z_scr[h] = _dot_nt(qs[h], k)
        for h in range(2):
            z = z_scr[h]
            sp = jnp.maximum(z, 0.0) + jnp.log(1.0 + jnp.exp(-jnp.abs(z)))
            if masked:
                sp = jnp.where(strict, sp, 0.0)
            hi = sp.astype(BF16)
            hi_scr[h] = hi
            lo_scr[h] = (sp - hi.astype(F32)).astype(BF16)
            z_scr[h] = z - sp
            off = run_scr[h]
            for s in reversed(range(nsub)):
                off_scr[h, s] = off
                rowsum = jnp.sum(sp[:, s * sub:(s + 1) * sub], axis=-1, keepdims=True)
                off = off + jnp.broadcast_to(rowsum, (blk, LANES))
            run_scr[h] = off
        for h in range(2):
            for s in range(nsub):
                sl = slice(s * sub, (s + 1) * sub)
                cs_scr[h, :, sl] = _dot(hi_scr[h, :, sl], later) + _dot(lo_scr[h, :, sl], later)
        for h in range(2):
            for s in range(nsub):
                sl = slice(s * sub, (s + 1) * sub)
                off = jnp.concatenate([off_scr[h, s]] * (sub // LANES), axis=1)
                att = jnp.exp(z_scr[h, :, sl] - cs_scr[h, :, sl] - off)
                if masked:
                    att = jnp.where(strict[:, sl], att, 0.0)
                att_scr[h, :, sl] = att.astype(BF16)
        for h in range(2):
            acc_scr[h] += _dot(att_scr[h], v)

    tile(qi, True)

    def body(jj, carry):
        tile(qi - 1 - jj, False)
        return carry

    lax.fori_loop(0, qi, body, 0)
    lane = lax.broadcasted_iota(jnp.int32, (blk, LANES), 1)
    o_ref[0] = jnp.where(lane < HEAD_DIM, acc_scr[0], acc_scr[1]).astype(o_ref.dtype)


def _stick_attention(u3, *, blk=512, sub=256):
    B, S, N3 = u3.shape
    D = N3 // 3
    npair = D // LANES
    return pl.pallas_call(
        functools.partial(_stick_kernel, blk=blk, sub=sub),
        grid=(B, npair, S // blk),
        in_specs=[
            pl.BlockSpec((1, blk, LANES), lambda b, h, i: (b, i, h)),
            pl.BlockSpec((1, S, LANES), lambda b, h, i: (b, 0, npair + h)),
            pl.BlockSpec((1, S, LANES), lambda b, h, i: (b, 0, 2 * npair + h)),
        ],
        out_specs=pl.BlockSpec((1, blk, LANES), lambda b, h, i: (b, i, h)),
        out_shape=jax.ShapeDtypeStruct((B, S, D), BF16),
        scratch_shapes=[pltpu.VMEM((2, blk, blk), F32), pltpu.VMEM((2, blk, blk), BF16),
                        pltpu.VMEM((2, blk, blk), BF16), pltpu.VMEM((2, blk, blk), F32),
                        pltpu.VMEM((2, blk, blk), BF16),
                        pltpu.VMEM((2, blk // sub, blk, LANES), F32),
                        pltpu.VMEM((2, blk, LANES), F32), pltpu.VMEM((2, blk, LANES), F32)],
        compiler_params=_cparams("parallel", "parallel", "arbitrary"),
    )(u3, u3, u3)


def _rope_tables(positions):
    half = HEAD_DIM // 2
    inv = ROPE_THETA ** (-jnp.arange(0, HEAD_DIM, 2, dtype=F32) / HEAD_DIM)
    ang = positions.astype(F32).reshape(-1, 1) * inv
    cos = jnp.cos(ang)
    sin = jnp.sin(ang)
    reps = LANES // HEAD_DIM
    cos_t = jnp.tile(jnp.concatenate([cos, cos], axis=1), (1, reps))
    sin_t = jnp.tile(jnp.concatenate([-sin, sin], axis=1), (1, reps))
    return cos_t, sin_t


def kernel(x, p, positions, ln_mix_pre, ln_mix_post, ln_ffn_pre, ln_ffn_post, ln_ple,
           w_ffn_in, w_ffn_out, w_ple_gate, b_ple_gate, w_ple_proj,
           a_w_in, a_conv_w, a_conv_b, a_gate_r_w, a_gate_r_b, a_gate_i_w, a_gate_i_b,
           a_lambda, a_w_out, b_w_qkv, b_w_out, c_w_qkv, c_w_out,
           d_w_qkv, d_lambda_q1, d_lambda_k1, d_lambda_q2, d_lambda_k2, d_subln, d_w_out):
    B, S, D = x.shape
    depth = p.shape[0]
    T = B * S
    cos_t, sin_t = _rope_tables(positions)
    assert all(w == d * DIL_BLOCK and S % w == 0 for w, d in DIL_PATTERNS)
    band = _dilated_band()
    bf = lambda w: w.astype(BF16)
    vec = lambda v: v.reshape(1, -1)

    x2 = x.reshape(T, D)
    for i in range(depth):
        kind, j = i % 4, i // 4
        gpre, gpost = vec(ln_mix_pre[i]), vec(ln_mix_post[i])
        if kind == 0:
            u = _norm_proj(x2, gpre, bf(a_w_in[j]), cos_t, sin_t,
                           rope=False, qscale=False, out_dtype=F32)
            x2 = _rglru_core(
                u.reshape(B, S, -1), x2.reshape(B, S, D), a_conv_w[j], vec(a_conv_b[j]),
                bf(_pair_block_diag(a_gate_r_w[j])), vec(a_gate_r_b[j]),
                bf(_pair_block_diag(a_gate_i_w[j])), vec(a_gate_i_b[j]),
                vec(a_lambda[j]), bf(a_w_out[j]), gpost).reshape(T, D)
        elif kind == 1:
            u = _norm_proj(x2, gpre, bf(b_w_qkv[j]), cos_t, sin_t,
                           rope=True, qscale=True, out_dtype=BF16)
            parts = [_dilated_pattern(u.reshape(B, S, -1), band, d) for _, d in DIL_PATTERNS]
            x2 = _merge_outproj([o for o, _ in parts], [l for _, l in parts],
                                bf(b_w_out[j]), x2, gpost)
        else:
            if kind == 2:
                u = _norm_proj(x2, gpre, bf(c_w_qkv[j]), cos_t, sin_t,
                               rope=False, qscale=True, out_dtype=BF16)
                o = _stick_attention(u.reshape(B, S, -1))
                w_o = c_w_out[j]
            else:
                u = _norm_proj(x2, gpre, bf(d_w_qkv[j]), cos_t, sin_t,
                               rope=True, qscale=True, out_dtype=BF16)
                lam_init = 0.8 - 0.6 * math.exp(-0.3 * i)
                o = _diff_attention(u.reshape(B, S, -1), vec(d_lambda_q1[j]), vec(d_lambda_k1[j]),
                                    vec(d_lambda_q2[j]), vec(d_lambda_k2[j]), vec(d_subln[j]),
                                    lam_init)
                w_o = d_w_out[j]
            x2 = _outproj(o.reshape(T, D), bf(w_o), x2, gpost)
        x2 = _ffn_ple(x2, vec(ln_ffn_pre[i]), bf(w_ffn_in[i]), bf(w_ffn_out[i]),
                      vec(ln_ffn_post[i]), p[i].reshape(T, -1), bf(w_ple_gate[i]),
                      vec(b_ple_gate[i]), bf(w_ple_proj[i]), vec(ln_ple[i]))
    return x2.reshape(B, S, D)
```

```python
import functools
import math

import numpy as np
import jax
import jax.numpy as jnp
from jax import lax
from jax.experimental import pallas as pl
from jax.experimental.pallas import tpu as pltpu

F32 = jnp.float32
BF16 = jnp.bfloat16

LANES = 128
HEAD_DIM = 64
QK_SCALE = HEAD_DIM ** -0.5
LOG2_E = math.log2(math.e)
ROPE_THETA = 10000.0
NORM_EPS = 1e-6
SUBLN_EPS = 1e-5
CONV_WIDTH = 4
RGLRU_C = 8.0
DIL_PATTERNS = ((128, 1), (512, 4), (2048, 16))
DIL_BLOCK = 128
VMEM_LIMIT = 56 * 1024 * 1024
NEG_INF = float("-inf")


def _cparams(*sem):
    return pltpu.CompilerParams(dimension_semantics=sem, vmem_limit_bytes=VMEM_LIMIT)


def _rms(v, g, eps):
    return v * lax.rsqrt(jnp.mean(v * v, axis=-1, keepdims=True) + eps) * g


def _dot(a, b):
    return jnp.dot(a, b, preferred_element_type=F32)


def _dot_nt(a, b):
    return lax.dot_general(a, b, (((1,), (1,)), ((), ())), preferred_element_type=F32)


def _rope_tile(t, cos, sin_signed):
    lane = lax.broadcasted_iota(jnp.int32, t.shape, 1)
    first_half = (lane % HEAD_DIM) < (HEAD_DIM // 2)
    partner = jnp.where(first_half,
                        pltpu.roll(t, LANES - HEAD_DIM // 2, 1),
                        pltpu.roll(t, HEAD_DIM // 2, 1))
    return t * cos + partner * sin_signed


def _proj_kernel(x_ref, g_ref, w_ref, cos_ref, sin_ref, o_ref, h_scr, *, rope, qscale):
    j = pl.program_id(1)

    @pl.when(j == 0)
    def _():
        h_scr[...] = _rms(x_ref[...], g_ref[...], NORM_EPS).astype(BF16)

    acc = _dot(h_scr[...], w_ref[...])
    if qscale is not None:
        acc = acc * jnp.where(j == 0, qscale, 1.0).astype(F32)
    if not rope:
        o_ref[...] = acc.astype(o_ref.dtype)
        return

    @pl.when(j < 2)
    def _():
        cos = cos_ref[...]
        sin = sin_ref[...]
        for c in range(acc.shape[1] // LANES):
            sl = slice(c * LANES, (c + 1) * LANES)
            o_ref[:, sl] = _rope_tile(acc[:, sl], cos, sin).astype(o_ref.dtype)

    @pl.when(j >= 2)
    def _():
        o_ref[...] = acc.astype(o_ref.dtype)


def _norm_proj(x2, g, w, cos, sin, *, rope, qscale, out_dtype, tm=512, tn=1024):
    T, D = x2.shape
    N = w.shape[1]
    return pl.pallas_call(
        functools.partial(_proj_kernel, rope=rope, qscale=qscale),
        grid=(T // tm, N // tn),
        in_specs=[
            pl.BlockSpec((tm, D), lambda i, j: (i, 0)),
            pl.BlockSpec((1, D), lambda i, j: (0, 0)),
            pl.BlockSpec((D, tn), lambda i, j: (0, j)),
            pl.BlockSpec((tm, LANES), lambda i, j: (i, 0)),
            pl.BlockSpec((tm, LANES), lambda i, j: (i, 0)),
        ],
        out_specs=pl.BlockSpec((tm, tn), lambda i, j: (i, j)),
        out_shape=jax.ShapeDtypeStruct((T, N), out_dtype),
        scratch_shapes=[pltpu.VMEM((tm, D), BF16)],
        compiler_params=_cparams("parallel", "arbitrary"),
    )(x2, g, w, cos, sin)


def _outproj_kernel(a_ref, w_ref, x_ref, g_ref, o_ref):
    y = _dot(a_ref[...], w_ref[...])
    o_ref[...] = x_ref[...] + _rms(y, g_ref[...], NORM_EPS)


def _outproj(a2, w, x2, g, *, tm=512):
    T, D = x2.shape
    K = a2.shape[1]
    return pl.pallas_call(
        _outproj_kernel,
        grid=(T // tm,),
        in_specs=[
            pl.BlockSpec((tm, K), lambda i: (i, 0)),
            pl.BlockSpec((K, D), lambda i: (0, 0)),
            pl.BlockSpec((tm, D), lambda i: (i, 0)),
            pl.BlockSpec((1, D), lambda i: (0, 0)),
        ],
        out_specs=pl.BlockSpec((tm, D), lambda i: (i, 0)),
        out_shape=jax.ShapeDtypeStruct((T, D), F32),
        compiler_params=_cparams("parallel"),
    )(a2, w, x2, g)


def _ffn_ple_kernel(x_ref, gpre_ref, w1_ref, w2_ref, wo_ref, gpost_ref,
                    p_ref, wg_ref, bg_ref, wp_ref, gple_ref, o_ref, h_scr, acc_scr):
    f = pl.program_id(1)

    @pl.when(f == 0)
    def _():
        h_scr[...] = _rms(x_ref[...], gpre_ref[...], NORM_EPS).astype(BF16)
        acc_scr[...] = jnp.zeros_like(acc_scr)

    h = h_scr[...]
    u1 = _dot(h, w1_ref[...])
    u2 = _dot(h, w2_ref[...])
    act = (u1 * jax.nn.sigmoid(u1) * u2).astype(BF16)
    acc_scr[...] += _dot(act, wo_ref[...])

    @pl.when(f == pl.num_programs(1) - 1)
    def _():
        x1 = x_ref[...] + _rms(acc_scr[...], gpost_ref[...], NORM_EPS)
        gate = jax.nn.sigmoid(_dot(x1.astype(BF16), wg_ref[...]) + bg_ref[...])
        proj = _dot(p_ref[...].astype(BF16), wp_ref[...])
        o_ref[...] = x1 + _rms(gate * proj, gple_ref[...], NORM_EPS)


def _ffn_ple(x2, gpre, w_in, w_out, gpost, p2, wg, bg, wp, gple, *, tm=1024, tf=256):
    T, D = x2.shape
    Fh = w_out.shape[0]
    nf = Fh // tf
    P = p2.shape[1]
    row = lambda i, f: (i, 0)
    const = lambda i, f: (0, 0)
    return pl.pallas_call(
        _ffn_ple_kernel,
        grid=(T // tm, nf),
        in_specs=[
            pl.BlockSpec((tm, D), row),
            pl.BlockSpec((1, D), const),
            pl.BlockSpec((D, tf), lambda i, f: (0, f)),
            pl.BlockSpec((D, tf), lambda i, f: (0, f + nf)),
            pl.BlockSpec((tf, D), lambda i, f: (f, 0)),
            pl.BlockSpec((1, D), const),
            pl.BlockSpec((tm, P), row),
            pl.BlockSpec((D, D), const),
            pl.BlockSpec((1, D), const),
            pl.BlockSpec((P, D), const),
            pl.BlockSpec((1, D), const),
        ],
        out_specs=pl.BlockSpec((tm, D), row),
        out_shape=jax.ShapeDtypeStruct((T, D), F32),
        scratch_shapes=[pltpu.VMEM((tm, D), BF16), pltpu.VMEM((tm, D), F32)],
        compiler_params=_cparams("parallel", "arbitrary"),
    )(x2, gpre, w_in, w_in, w_out, gpost, p2, wg, bg, wp, gple)


def _gelu_tanh(v):
    return 0.5 * v * (1.0 + jnp.tanh(math.sqrt(2.0 / math.pi) * (v + 0.044715 * v * v * v)))


def _rglru_kernel(y_ref, xb_ref, x_ref, cw_ref, cb_ref, wr_ref, br_ref, wi_ref, bi_ref,
                  lam_ref, wo_ref, gpost_ref, o_ref, ext_scr, h_scr, *, tc):
    t = pl.program_id(1)
    tail = 8
    width = xb_ref.shape[2]

    @pl.when(t == 0)
    def _():
        ext_scr[0:tail, :] = jnp.zeros((tail, width), F32)
        h_scr[...] = jnp.zeros_like(h_scr)

    ext_scr[tail:tail + tc, :] = xb_ref[0]
    xc = cb_ref[...] + ext_scr[tail:tail + tc, :] * cw_ref[CONV_WIDTH - 1:CONV_WIDTH, :]
    for back in range(1, CONV_WIDTH):
        tap = CONV_WIDTH - 1 - back
        xc = xc + ext_scr[tail - back:tail - back + tc, :] * cw_ref[tap:tap + 1, :]
    ext_scr[0:tail, :] = ext_scr[tc:tc + tail, :]

    r_parts, i_parts = [], []
    for c in range(width // LANES):
        xcb = xc[:, c * LANES:(c + 1) * LANES].astype(BF16)
        r_parts.append(_dot(xcb, wr_ref[c]))
        i_parts.append(_dot(xcb, wi_ref[c]))
    r = jax.nn.sigmoid(jnp.concatenate(r_parts, axis=1) + br_ref[...])
    ig = jax.nn.sigmoid(jnp.concatenate(i_parts, axis=1) + bi_ref[...])

    nl = -lam_ref[...]
    softplus_nl = jnp.maximum(nl, 0.0) + jnp.log1p(jnp.exp(-jnp.abs(nl)))
    log_a = (-RGLRU_C) * r * softplus_nl
    a = jnp.exp(log_a)
    b = jnp.sqrt(-jnp.tanh(log_a) * (a * a + 1.0)) * (ig * xc)

    rows = lax.broadcasted_iota(jnp.int32, (tc, width), 0)
    s = 1
    while s < tc:
        keep = rows >= s
        a_sh = jnp.where(keep, pltpu.roll(a, s, 0), 1.0)
        b_sh = jnp.where(keep, pltpu.roll(b, s, 0), 0.0)
        b = a * b_sh + b
        a = a * a_sh
        s *= 2
    hs = b + a * h_scr[...]
    h_scr[...] = hs[tc - 1:tc, :]

    z = (hs * _gelu_tanh(y_ref[0])).astype(BF16)
    o_ref[0] = x_ref[0] + _rms(_dot(z, wo_ref[...]), gpost_ref[...], NORM_EPS)


def _rglru_core(u3, x3, cw, cb, wr, br, wi, bi, lam, wo, gpost, *, tc=256):
    B, S, D = x3.shape
    W = u3.shape[2] // 2
    nl = W // LANES
    const2 = lambda b, t: (0, 0)
    return pl.pallas_call(
        functools.partial(_rglru_kernel, tc=tc),
        grid=(B, S // tc),
        in_specs=[
            pl.BlockSpec((1, tc, W), lambda b, t: (b, t, 0)),
            pl.BlockSpec((1, tc, W), lambda b, t: (b, t, 1)),
            pl.BlockSpec((1, tc, D), lambda b, t: (b, t, 0)),
            pl.BlockSpec((CONV_WIDTH, W), const2),
            pl.BlockSpec((1, W), const2),
            pl.BlockSpec((nl, LANES, LANES), lambda b, t: (0, 0, 0)),
            pl.BlockSpec((1, W), const2),
            pl.BlockSpec((nl, LANES, LANES), lambda b, t: (0, 0, 0)),
            pl.BlockSpec((1, W), const2),
            pl.BlockSpec((1, W), const2),
            pl.BlockSpec((W, D), const2),
            pl.BlockSpec((1, D), const2),
        ],
        out_specs=pl.BlockSpec((1, tc, D), lambda b, t: (b, t, 0)),
        out_shape=jax.ShapeDtypeStruct((B, S, D), F32),
        scratch_shapes=[pltpu.VMEM((tc + 8, W), F32), pltpu.VMEM((1, W), F32)],
        compiler_params=_cparams("parallel", "arbitrary"),
    )(u3, u3, x3, cw, cb, wr, br, wi, bi, lam, wo, gpost)


def _pair_block_diag(w):
    n, d, _ = w.shape
    w = w.reshape(n // 2, 2, d, d)
    z = jnp.zeros_like(w[:, 0])
    top = jnp.concatenate([w[:, 0], z], axis=2)
    bot = jnp.concatenate([z, w[:, 1]], axis=2)
    return jnp.concatenate([top, bot], axis=1)


def _split_pair(q):
    lane = lax.broadcasted_iota(jnp.int32, q.shape, 1)
    zero = jnp.zeros_like(q)
    return jnp.where(lane < HEAD_DIM, q, zero), jnp.where(lane >= HEAD_DIM, q, zero)


def _dilated_kernel(q_ref, kp_ref, kc_ref, vp_ref, vc_ref, band_ref, o_ref, lse_ref,
                    s_scr, p_scr, *, blk):
    n = pl.program_id(1)
    band = band_ref[...]
    bias = jnp.concatenate(
        [band[:, :blk] + jnp.where(n == 0, NEG_INF, 0.0).astype(F32), band[:, blk:]], axis=1)
    lane = lax.broadcasted_iota(jnp.int32, (blk, LANES), 1)
    low = lane < HEAD_DIM
    npair = q_ref.shape[2] // LANES
    pair = lambda pr: slice(pr * LANES, (pr + 1) * LANES)

    for pr in range(npair):
        kcat = jnp.concatenate([kp_ref[0, :, pair(pr)], kc_ref[0, :, pair(pr)]], axis=0)
        for c, qh in enumerate(_split_pair(q_ref[0, :, pair(pr)])):
            s_scr[2 * pr + c] = _dot_nt(qh, kcat) + bias

    row_max = []
    for c in range(2 * npair):
        s = s_scr[c]
        m = jnp.max(s, axis=-1, keepdims=True)
        p_scr[c] = jnp.exp(s - m).astype(BF16)
        row_max.append(m)

    ones = jnp.ones((2 * blk, LANES), BF16)
    low2 = lax.broadcasted_iota(jnp.int32, (2 * blk, LANES), 1) < HEAD_DIM
    for pr in range(npair):
        vcat = jnp.concatenate([vp_ref[0, :, pair(pr)], vc_ref[0, :, pair(pr)]], axis=0)
        a0 = _dot(p_scr[2 * pr], jnp.where(low2, vcat, ones))
        a1 = _dot(p_scr[2 * pr + 1], jnp.where(low2, ones, vcat))
        num = jnp.where(low, a0, a1)
        den = pltpu.roll(jnp.where(low, a1, a0), HEAD_DIM, 1)
        m = jnp.where(low, row_max[2 * pr], row_max[2 * pr + 1])
        o_ref[0, :, pair(pr)] = (num / den).astype(o_ref.dtype)
        lse_ref[0, :, pair(pr)] = m + jnp.log(den)


def _dilated_band():
    blk = DIL_BLOCK
    i = np.arange(blk)[:, None]
    j = np.arange(2 * blk)[None, :]
    return jnp.asarray(np.where((j >= i) & (j <= i + blk), 0.0, -np.inf).astype(np.float32))


def _dilated_pattern(u3, band, dilation):
    B, S, N3 = u3.shape
    D = N3 // 3
    blk = DIL_BLOCK
    d = dilation
    uv = u3.reshape(B, S // d, d * N3)
    prev = lambda n: jnp.maximum(n - 1, 0)
    out_spec = pl.BlockSpec((1, blk, D), lambda b, n, r: (b, n, r))
    o, lse = pl.pallas_call(
        functools.partial(_dilated_kernel, blk=blk),
        grid=(B, S // (d * blk), d),
        in_specs=[
            pl.BlockSpec((1, blk, D), lambda b, n, r: (b, n, 3 * r)),
            pl.BlockSpec((1, blk, D), lambda b, n, r: (b, prev(n), 3 * r + 1)),
            pl.BlockSpec((1, blk, D), lambda b, n, r: (b, n, 3 * r + 1)),
            pl.BlockSpec((1, blk, D), lambda b, n, r: (b, prev(n), 3 * r + 2)),
            pl.BlockSpec((1, blk, D), lambda b, n, r: (b, n, 3 * r + 2)),
            pl.BlockSpec((blk, 2 * blk), lambda b, n, r: (0, 0)),
        ],
        out_specs=[out_spec, out_spec],
        out_shape=[jax.ShapeDtypeStruct((B, S // d, d * D), BF16),
                   jax.ShapeDtypeStruct((B, S // d, d * D), F32)],
        scratch_shapes=[pltpu.VMEM((D // HEAD_DIM, blk, 2 * blk), F32),
                        pltpu.VMEM((D // HEAD_DIM, blk, 2 * blk), BF16)],
        compiler_params=_cparams("parallel", "parallel", "parallel"),
    )(uv, uv, uv, uv, uv, band)
    return o.reshape(B * S, D), lse.reshape(B * S, D)


def _merge_outproj_kernel(o1_ref, o2_ref, o3_ref, l1_ref, l2_ref, l3_ref, w_ref, x_ref, g_ref, o_ref):
    l1, l2, l3 = l1_ref[...], l2_ref[...], l3_ref[...]
    m = jnp.maximum(jnp.maximum(l1, l2), l3)
    w1, w2, w3 = jnp.exp(l1 - m), jnp.exp(l2 - m), jnp.exp(l3 - m)
    num = w1 * o1_ref[...].astype(F32) + w2 * o2_ref[...].astype(F32) + w3 * o3_ref[...].astype(F32)
    a = (num / (w1 + w2 + w3)).astype(BF16)
    o_ref[...] = x_ref[...] + _rms(_dot(a, w_ref[...]), g_ref[...], NORM_EPS)


def _merge_outproj(os_, lses, w, x2, g, *, tm=512):
    T, D = x2.shape
    row = pl.BlockSpec((tm, D), lambda i: (i, 0))
    return pl.pallas_call(
        _merge_outproj_kernel,
        grid=(T // tm,),
        in_specs=[row] * 6 + [pl.BlockSpec((D, D), lambda i: (0, 0)), row,
                              pl.BlockSpec((1, D), lambda i: (0, 0))],
        out_specs=row,
        out_shape=jax.ShapeDtypeStruct((T, D), F32),
        compiler_params=_cparams("parallel"),
    )(*os_, *lses, w, x2, g)


def _diff_kernel(q_ref, k_ref, v_ref, lq1_ref, lk1_ref, lq2_ref, lk2_ref, g_ref, o_ref,
                 s_scr, p_scr, m_scr, alpha_scr, acc_scr, *, blk, lam_init):
    qi = pl.program_id(2)
    qs = _split_pair(q_ref[0])
    m_scr[...] = jnp.full(m_scr.shape, NEG_INF, F32)
    acc_scr[...] = jnp.zeros(acc_scr.shape, F32)
    lane_tiles = blk // LANES

    def tile(j, masked):
        start = pl.multiple_of(j * blk, blk)
        k = k_ref[0, pl.ds(start, blk), :]
        v1 = jnp.concatenate([v_ref[0, pl.ds(start, blk), :], jnp.ones((blk, LANES), BF16)], axis=1)
        for c in range(2):
            s = _dot_nt(qs[c], k)
            if masked:
                row = lax.broadcasted_iota(jnp.int32, (blk, blk), 0)
                col = lax.broadcasted_iota(jnp.int32, (blk, blk), 1)
                s = jnp.where(col <= row, s, NEG_INF)
            s_scr[c] = s
        for c in range(2):
            m_old = m_scr[c]
            row_max = jnp.max(s_scr[c], axis=-1, keepdims=True)
            m_new = jnp.maximum(m_old, jnp.broadcast_to(row_max, (blk, LANES)))
            alpha_scr[c] = jnp.exp(m_old - m_new)
            m_scr[c] = m_new
        for c in range(2):
            m = jnp.concatenate([m_scr[c]] * lane_tiles, axis=1)
            p_scr[c] = jnp.exp(s_scr[c] - m).astype(BF16)
        for c in range(2):
            alpha = jnp.concatenate([alpha_scr[c]] * 2, axis=1)
            acc_scr[c] = alpha * acc_scr[c] + _dot(p_scr[c], v1)

    tile(qi, True)

    def body(j, carry):
        tile(j, False)
        return carry

    lax.fori_loop(0, qi, body, 0)

    lam = (jnp.exp(jnp.sum(lq1_ref[...] * lk1_ref[...], axis=-1, keepdims=True))
           - jnp.exp(jnp.sum(lq2_ref[...] * lk2_ref[...], axis=-1, keepdims=True)) + lam_init)
    a0, a1 = acc_scr[0], acc_scr[1]
    o = a0[:, :LANES] / a0[:, LANES:] - lam * (a1[:, :LANES] / a1[:, LANES:])
    o_ref[0] = (_rms(o, g_ref[...], SUBLN_EPS) * (1.0 - lam_init)).astype(o_ref.dtype)


def _diff_attention(u3, lq1, lk1, lq2, lk2, subln, lam_init, *, blk=512):
    B, S, N3 = u3.shape
    D = N3 // 3
    npair = D // LANES
    vec = lambda n: pl.BlockSpec((1, n), lambda b, h, i: (0, 0))
    return pl.pallas_call(
        functools.partial(_diff_kernel, blk=blk, lam_init=lam_init),
        grid=(B, npair, S // blk),
        in_specs=[
            pl.BlockSpec((1, blk, LANES), lambda b, h, i: (b, i, h)),
            pl.BlockSpec((1, S, LANES), lambda b, h, i: (b, 0, npair + h)),
            pl.BlockSpec((1, S, LANES), lambda b, h, i: (b, 0, 2 * npair + h)),
            vec(HEAD_DIM), vec(HEAD_DIM), vec(HEAD_DIM), vec(HEAD_DIM), vec(LANES),
        ],
        out_specs=pl.BlockSpec((1, blk, LANES), lambda b, h, i: (b, i, h)),
        out_shape=jax.ShapeDtypeStruct((B, S, D), BF16),
        scratch_shapes=[pltpu.VMEM((2, blk, blk), F32), pltpu.VMEM((2, blk, blk), BF16),
                        pltpu.VMEM((2, blk, LANES), F32), pltpu.VMEM((2, blk, LANES), F32),
                        pltpu.VMEM((2, blk, 2 * LANES), F32)],
        compiler_params=_cparams("parallel", "parallel", "arbitrary"),
    )(u3, u3, u3, lq1, lk1, lq2, lk2, subln)


def _stick_kernel(q_ref, k_ref, v_ref, o_ref, z_scr, hi_scr, cs_scr, att_scr,
                  run_scr, acc_scr, *, blk, sub):
    qi = pl.program_id(2)
    qs = _split_pair(q_ref[0])
    nsub = blk // sub
    r2 = lax.broadcasted_iota(jnp.int32, (sub, sub), 0)
    c2 = lax.broadcasted_iota(jnp.int32, (sub, sub), 1)
    later = jnp.where(c2 < r2, 1.0, 0.0).astype(BF16)
    run_scr[...] = jnp.zeros(run_scr.shape, F32)
    acc_scr[...] = jnp.zeros(acc_scr.shape, F32)
    sign_bit = jnp.int32(-2 ** 31)

    def tile(j, masked):
        start = pl.multiple_of(j * blk, blk)
        k = k_ref[0, pl.ds(start, blk), :]
        v = v_ref[0, pl.ds(start, blk), :]
        if masked:
            row = lax.broadcasted_iota(jnp.int32, (blk, blk), 0)
            col = lax.broadcasted_iota(jnp.int32, (blk, blk), 1)
            strict = col < row

        def softplus_stage(h):
            off = run_scr[h]
            for s in reversed(range(nsub)):
                sl = slice(s * sub, (s + 1) * sub)
                z = z_scr[h, :, sl]
                neg_abs = lax.bitcast_convert_type(
                    lax.bitcast_convert_type(z, jnp.int32) | sign_bit, F32)
                sp = jnp.maximum(z, 0.0) + jnp.log2(1.0 + jnp.exp2(neg_abs))
                if masked:
                    sp = jnp.where(strict[:, sl], sp, 0.0)
                hi_scr[h, :, sl] = sp.astype(BF16)
                z_scr[h, :, sl] = z - sp - jnp.concatenate([off] * (sub // LANES), axis=1)
                rowsum = jnp.sum(sp, axis=-1, keepdims=True)
                off = off + jnp.broadcast_to(rowsum, off.shape)
            run_scr[h] = off

        def suffix_stage(h):
            for s in range(nsub):
                sl = slice(s * sub, (s + 1) * sub)
                cs_scr[h, :, sl] = _dot(hi_scr[h, :, sl], later)

        def weight_stage(h):
            att = jnp.exp2(z_scr[h] - cs_scr[h])
            if masked:
                att = jnp.where(strict, att, 0.0)
            att_scr[h] = att.astype(BF16)
            acc_scr[h] += _dot(att_scr[h], v)

        for h in range(2):
            z_scr[h] = _dot_nt(qs[h], k)
        softplus_stage(0)
        suffix_stage(0)
        softplus_stage(1)
        suffix_stage(1)
        weight_stage(0)
        weight_stage(1)

    tile(qi, True)

    def body(jj, carry):
        tile(qi - 1 - jj, False)
        return carry

    lax.fori_loop(0, qi, body, 0)
    lane = lax.broadcasted_iota(jnp.int32, (blk, LANES), 1)
    o_ref[0] = jnp.where(lane < HEAD_DIM, acc_scr[0], acc_scr[1]).astype(o_ref.dtype)


def _stick_attention(u3, *, blk=512, sub=256):
    B, S, N3 = u3.shape
    D = N3 // 3
    npair = D // LANES
    return pl.pallas_call(
        functools.partial(_stick_kernel, blk=blk, sub=sub),
        grid=(B, npair, S // blk),
        in_specs=[
            pl.BlockSpec((1, blk, LANES), lambda b, h, i: (b, i, h)),
            pl.BlockSpec((1, S, LANES), lambda b, h, i: (b, 0, npair + h)),
            pl.BlockSpec((1, S, LANES), lambda b, h, i: (b, 0, 2 * npair + h)),
        ],
        out_specs=pl.BlockSpec((1, blk, LANES), lambda b, h, i: (b, i, h)),
        out_shape=jax.ShapeDtypeStruct((B, S, D), BF16),
        scratch_shapes=[pltpu.VMEM((2, blk, blk), F32), pltpu.VMEM((2, blk, blk), BF16),
                        pltpu.VMEM((2, blk, blk), F32), pltpu.VMEM((2, blk, blk), BF16),
                        pltpu.VMEM((2, blk, LANES), F32), pltpu.VMEM((2, blk, LANES), F32)],
        compiler_params=_cparams("parallel", "parallel", "arbitrary"),
    )(u3, u3, u3)


def _rope_tables(positions):
    half = HEAD_DIM // 2
    inv = ROPE_THETA ** (-jnp.arange(0, HEAD_DIM, 2, dtype=F32) / HEAD_DIM)
    ang = positions.astype(F32).reshape(-1, 1) * inv
    cos = jnp.cos(ang)
    sin = jnp.sin(ang)
    reps = LANES // HEAD_DIM
    cos_t = jnp.tile(jnp.concatenate([cos, cos], axis=1), (1, reps))
    sin_t = jnp.tile(jnp.concatenate([-sin, sin], axis=1), (1, reps))
    return cos_t, sin_t


def kernel(x, p, positions, ln_mix_pre, ln_mix_post, ln_ffn_pre, ln_ffn_post, ln_ple,
           w_ffn_in, w_ffn_out, w_ple_gate, b_ple_gate, w_ple_proj,
           a_w_in, a_conv_w, a_conv_b, a_gate_r_w, a_gate_r_b, a_gate_i_w, a_gate_i_b,
           a_lambda, a_w_out, b_w_qkv, b_w_out, c_w_qkv, c_w_out,
           d_w_qkv, d_lambda_q1, d_lambda_k1, d_lambda_q2, d_lambda_k2, d_subln, d_w_out):
    B, S, D = x.shape
    depth = p.shape[0]
    T = B * S
    cos_t, sin_t = _rope_tables(positions)
    assert all(w == d * DIL_BLOCK and S % w == 0 for w, d in DIL_PATTERNS)
    band = _dilated_band()
    bf = lambda w: w.astype(BF16)
    vec = lambda v: v.reshape(1, -1)

    x2 = x.reshape(T, D)
    for i in range(depth):
        kind, j = i % 4, i // 4
        gpre, gpost = vec(ln_mix_pre[i]), vec(ln_mix_post[i])
        if kind == 0:
            u = _norm_proj(x2, gpre, bf(a_w_in[j]), cos_t, sin_t,
                           rope=False, qscale=None, out_dtype=F32)
            x2 = _rglru_core(
                u.reshape(B, S, -1), x2.reshape(B, S, D), a_conv_w[j], vec(a_conv_b[j]),
                bf(_pair_block_diag(a_gate_r_w[j])), vec(a_gate_r_b[j]),
                bf(_pair_block_diag(a_gate_i_w[j])), vec(a_gate_i_b[j]),
                vec(a_lambda[j]), bf(a_w_out[j]), gpost).reshape(T, D)
        elif kind == 1:
            u = _norm_proj(x2, gpre, bf(b_w_qkv[j]), cos_t, sin_t,
                           rope=True, qscale=QK_SCALE, out_dtype=BF16)
            parts = [_dilated_pattern(u.reshape(B, S, -1), band, d) for _, d in DIL_PATTERNS]
            x2 = _merge_outproj([o for o, _ in parts], [l for _, l in parts],
                                bf(b_w_out[j]), x2, gpost)
        else:
            if kind == 2:
                u = _norm_proj(x2, gpre, bf(c_w_qkv[j]), cos_t, sin_t,
                               rope=False, qscale=QK_SCALE * LOG2_E, out_dtype=BF16)
                o = _stick_attention(u.reshape(B, S, -1))
                w_o = c_w_out[j]
            else:
                u = _norm_proj(x2, gpre, bf(d_w_qkv[j]), cos_t, sin_t,
                               rope=True, qscale=QK_SCALE, out_dtype=BF16)
                lam_init = 0.8 - 0.6 * math.exp(-0.3 * i)
                o = _diff_attention(u.reshape(B, S, -1), vec(d_lambda_q1[j]), vec(d_lambda_k1[j]),
                                    vec(d_lambda_q2[j]), vec(d_lambda_k2[j]), vec(d_subln[j]),
                                    lam_init)
                w_o = d_w_out[j]
            x2 = _outproj(o.reshape(T, D), bf(w_o), x2, gpost)
        x2 = _ffn_ple(x2, vec(ln_ffn_pre[i]), bf(w_ffn_in[i]), bf(w_ffn_out[i]),
                      vec(ln_ffn_post[i]), p[i].reshape(T, -1), bf(w_ple_gate[i]),
                      vec(b_ple_gate[i]), bf(w_ple_proj[i]), vec(ln_ple[i]))
    return x2.reshape(B, S, D)
```

```python
import functools
import math

import numpy as np
import jax
import jax.numpy as jnp
from jax import lax
from jax.experimental import pallas as pl
from jax.experimental.pallas import tpu as pltpu

F32 = jnp.float32
BF16 = jnp.bfloat16

LANES = 128
HEAD_DIM = 64
QK_SCALE = HEAD_DIM ** -0.5
LOG2_E = math.log2(math.e)
ROPE_THETA = 10000.0
NORM_EPS = 1e-6
SUBLN_EPS = 1e-5
CONV_WIDTH = 4
RGLRU_C = 8.0
DIL_PATTERNS = ((128, 1), (512, 4), (2048, 16))
DIL_BLOCK = 128
VMEM_LIMIT = 56 * 1024 * 1024
NEG_INF = float("-inf")


def _cparams(*sem):
    return pltpu.CompilerParams(dimension_semantics=sem, vmem_limit_bytes=VMEM_LIMIT)


def _rms(v, g, eps):
    return v * lax.rsqrt(jnp.mean(v * v, axis=-1, keepdims=True) + eps) * g


def _dot(a, b):
    return jnp.dot(a, b, preferred_element_type=F32)


def _dot_nt(a, b):
    return lax.dot_general(a, b, (((1,), (1,)), ((), ())), preferred_element_type=F32)


def _rope_tile(t, cos, sin_signed):
    lane = lax.broadcasted_iota(jnp.int32, t.shape, 1)
    first_half = (lane % HEAD_DIM) < (HEAD_DIM // 2)
    partner = jnp.where(first_half,
                        pltpu.roll(t, LANES - HEAD_DIM // 2, 1),
                        pltpu.roll(t, HEAD_DIM // 2, 1))
    return t * cos + partner * sin_signed


def _proj_kernel(x_ref, g_ref, w_ref, cos_ref, sin_ref, *refs, rope, qscale, strides):
    o_ref = refs[0]
    strided_refs = refs[1:1 + len(strides)]
    h_scr = refs[1 + len(strides)]
    stage = refs[2 + len(strides)] if strides else None
    j = pl.program_id(1)

    @pl.when(j == 0)
    def _():
        h_scr[...] = _rms(x_ref[...], g_ref[...], NORM_EPS).astype(BF16)

    acc = _dot(h_scr[...], w_ref[...])
    if qscale is not None:
        acc = acc * jnp.where(j == 0, qscale, 1.0).astype(F32)
    tm = acc.shape[0]
    chunks = [slice(c * LANES, (c + 1) * LANES) for c in range(acc.shape[1] // LANES)]

    def put(c, val):
        if strides:
            stage[c * tm:(c + 1) * tm, :] = val
        else:
            o_ref[:, chunks[c]] = val.astype(o_ref.dtype)

    def plain():
        for c, sl in enumerate(chunks):
            put(c, acc[:, sl])

    if not rope:
        plain()
    else:
        @pl.when(j < 2)
        def _():
            cos = cos_ref[...]
            sin = sin_ref[...]
            for c, sl in enumerate(chunks):
                put(c, _rope_tile(acc[:, sl], cos, sin))

        pl.when(j >= 2)(plain)

    if strides:
        for c, sl in enumerate(chunks):
            o_ref[:, sl] = stage[c * tm:(c + 1) * tm, :].astype(o_ref.dtype)
            for d, ref in zip(strides, strided_refs):
                for r in range(d):
                    ref[r, :, sl] = stage[pl.ds(c * tm + r, tm // d, stride=d), :].astype(ref.dtype)


def _norm_proj(x2, g, w, cos, sin, *, rope, qscale, out_dtype, strides=(), tm=1024, tn=1024):
    T, D = x2.shape
    N = w.shape[1]
    out_specs = [pl.BlockSpec((tm, tn), lambda i, j: (i, j))]
    out_shape = [jax.ShapeDtypeStruct((T, N), out_dtype)]
    for d in strides:
        out_specs.append(pl.BlockSpec((d, tm // d, tn), lambda i, j: (0, i, j)))
        out_shape.append(jax.ShapeDtypeStruct((d, T // d, N), out_dtype))
    scratch = [pltpu.VMEM((tm, D), BF16)]
    if strides:
        scratch.append(pltpu.VMEM((tn // LANES * tm, LANES), F32))
    outs = pl.pallas_call(
        functools.partial(_proj_kernel, rope=rope, qscale=qscale, strides=tuple(strides)),
        grid=(T // tm, N // tn),
        in_specs=[
            pl.BlockSpec((tm, D), lambda i, j: (i, 0)),
            pl.BlockSpec((1, D), lambda i, j: (0, 0)),
            pl.BlockSpec((D, tn), lambda i, j: (0, j)),
            pl.BlockSpec((tm, LANES), lambda i, j: (i, 0)),
            pl.BlockSpec((tm, LANES), lambda i, j: (i, 0)),
        ],
        out_specs=out_specs,
        out_shape=out_shape,
        scratch_shapes=scratch,
        compiler_params=_cparams("parallel", "arbitrary"),
    )(x2, g, w, cos, sin)
    return outs if strides else outs[0]


def _outproj_kernel(a_ref, w_ref, x_ref, g_ref, o_ref):
    y = _dot(a_ref[...], w_ref[...])
    o_ref[...] = x_ref[...] + _rms(y, g_ref[...], NORM_EPS)


def _outproj(a2, w, x2, g, *, tm=1024):
    T, D = x2.shape
    K = a2.shape[1]
    return pl.pallas_call(
        _outproj_kernel,
        grid=(T // tm,),
        in_specs=[
            pl.BlockSpec((tm, K), lambda i: (i, 0)),
            pl.BlockSpec((K, D), lambda i: (0, 0)),
            pl.BlockSpec((tm, D), lambda i: (i, 0)),
            pl.BlockSpec((1, D), lambda i: (0, 0)),
        ],
        out_specs=pl.BlockSpec((tm, D), lambda i: (i, 0)),
        out_shape=jax.ShapeDtypeStruct((T, D), F32),
        compiler_params=_cparams("parallel"),
    )(a2, w, x2, g)


def _ffn_ple_kernel(x_ref, gpre_ref, w1_ref, w2_ref, wo_ref, gpost_ref,
                    p_ref, wg_ref, bg_ref, wp_ref, gple_ref, o_ref, h_scr, acc_scr):
    f = pl.program_id(1)

    @pl.when(f == 0)
    def _():
        h_scr[...] = _rms(x_ref[...], gpre_ref[...], NORM_EPS).astype(BF16)
        acc_scr[...] = jnp.zeros_like(acc_scr)

    h = h_scr[...]
    u1 = _dot(h, w1_ref[...])
    u2 = _dot(h, w2_ref[...])
    act = (u1 * jax.nn.sigmoid(u1) * u2).astype(BF16)
    acc_scr[...] += _dot(act, wo_ref[...])

    @pl.when(f == pl.num_programs(1) - 1)
    def _():
        x1 = x_ref[...] + _rms(acc_scr[...], gpost_ref[...], NORM_EPS)
        gate = jax.nn.sigmoid(_dot(x1.astype(BF16), wg_ref[...]) + bg_ref[...])
        proj = _dot(p_ref[...].astype(BF16), wp_ref[...])
        o_ref[...] = x1 + _rms(gate * proj, gple_ref[...], NORM_EPS)


def _ffn_ple(x2, gpre, w_in, w_out, gpost, p2, wg, bg, wp, gple, *, tm=1024, tf=256):
    T, D = x2.shape
    Fh = w_out.shape[0]
    nf = Fh // tf
    P = p2.shape[1]
    row = lambda i, f: (i, 0)
    const = lambda i, f: (0, 0)
    return pl.pallas_call(
        _ffn_ple_kernel,
        grid=(T // tm, nf),
        in_specs=[
            pl.BlockSpec((tm, D), row),
            pl.BlockSpec((1, D), const),
            pl.BlockSpec((D, tf), lambda i, f: (0, f)),
            pl.BlockSpec((D, tf), lambda i, f: (0, f + nf)),
            pl.BlockSpec((tf, D), lambda i, f: (f, 0)),
            pl.BlockSpec((1, D), const),
            pl.BlockSpec((tm, P), row),
            pl.BlockSpec((D, D), const),
            pl.BlockSpec((1, D), const),
            pl.BlockSpec((P, D), const),
            pl.BlockSpec((1, D), const),
        ],
        out_specs=pl.BlockSpec((tm, D), row),
        out_shape=jax.ShapeDtypeStruct((T, D), F32),
        scratch_shapes=[pltpu.VMEM((tm, D), BF16), pltpu.VMEM((tm, D), F32)],
        compiler_params=_cparams("parallel", "arbitrary"),
    )(x2, gpre, w_in, w_in, w_out, gpost, p2, wg, bg, wp, gple)


def _gelu_tanh(v):
    return 0.5 * v * (1.0 + jnp.tanh(math.sqrt(2.0 / math.pi) * (v + 0.044715 * v * v * v)))


def _rglru_kernel(y_ref, xb_ref, x_ref, cw_ref, cb_ref, wr_ref, br_ref, wi_ref, bi_ref,
                  lam_ref, wo_ref, gpost_ref, o_ref, ext_scr, h_scr, *, tc):
    t = pl.program_id(1)
    tail = 8
    width = xb_ref.shape[2]

    @pl.when(t == 0)
    def _():
        ext_scr[0:tail, :] = jnp.zeros((tail, width), F32)
        h_scr[...] = jnp.zeros_like(h_scr)

    ext_scr[tail:tail + tc, :] = xb_ref[0]
    xc = cb_ref[...] + ext_scr[tail:tail + tc, :] * cw_ref[CONV_WIDTH - 1:CONV_WIDTH, :]
    for back in range(1, CONV_WIDTH):
        tap = CONV_WIDTH - 1 - back
        xc = xc + ext_scr[tail - back:tail - back + tc, :] * cw_ref[tap:tap + 1, :]
    ext_scr[0:tail, :] = ext_scr[tc:tc + tail, :]

    r_parts, i_parts = [], []
    for c in range(width // LANES):
        xcb = xc[:, c * LANES:(c + 1) * LANES].astype(BF16)
        r_parts.append(_dot(xcb, wr_ref[c]))
        i_parts.append(_dot(xcb, wi_ref[c]))
    r = jax.nn.sigmoid(jnp.concatenate(r_parts, axis=1) + br_ref[...])
    ig = jax.nn.sigmoid(jnp.concatenate(i_parts, axis=1) + bi_ref[...])

    nl = -lam_ref[...]
    softplus_nl = jnp.maximum(nl, 0.0) + jnp.log1p(jnp.exp(-jnp.abs(nl)))
    log_a = (-RGLRU_C) * r * softplus_nl
    a = jnp.exp(log_a)
    b = jnp.sqrt(-jnp.tanh(log_a) * (a * a + 1.0)) * (ig * xc)

    rows = lax.broadcasted_iota(jnp.int32, (tc, width), 0)
    s = 1
    while s < tc:
        keep = rows >= s
        a_sh = jnp.where(keep, pltpu.roll(a, s, 0), 1.0)
        b_sh = jnp.where(keep, pltpu.roll(b, s, 0), 0.0)
        b = a * b_sh + b
        a = a * a_sh
        s *= 2
    hs = b + a * h_scr[...]
    h_scr[...] = hs[tc - 1:tc, :]

    z = (hs * _gelu_tanh(y_ref[0])).astype(BF16)
    o_ref[0] = x_ref[0] + _rms(_dot(z, wo_ref[...]), gpost_ref[...], NORM_EPS)


def _rglru_core(u3, x3, cw, cb, wr, br, wi, bi, lam, wo, gpost, *, tc=256):
    B, S, D = x3.shape
    W = u3.shape[2] // 2
    nl = W // LANES
    const2 = lambda b, t: (0, 0)
    return pl.pallas_call(
        functools.partial(_rglru_kernel, tc=tc),
        grid=(B, S // tc),
        in_specs=[
            pl.BlockSpec((1, tc, W), lambda b, t: (b, t, 0)),
            pl.BlockSpec((1, tc, W), lambda b, t: (b, t, 1)),
            pl.BlockSpec((1, tc, D), lambda b, t: (b, t, 0)),
            pl.BlockSpec((CONV_WIDTH, W), const2),
            pl.BlockSpec((1, W), const2),
            pl.BlockSpec((nl, LANES, LANES), lambda b, t: (0, 0, 0)),
            pl.BlockSpec((1, W), const2),
            pl.BlockSpec((nl, LANES, LANES), lambda b, t: (0, 0, 0)),
            pl.BlockSpec((1, W), const2),
            pl.BlockSpec((1, W), const2),
            pl.BlockSpec((W, D), const2),
            pl.BlockSpec((1, D), const2),
        ],
        out_specs=pl.BlockSpec((1, tc, D), lambda b, t: (b, t, 0)),
        out_shape=jax.ShapeDtypeStruct((B, S, D), F32),
        scratch_shapes=[pltpu.VMEM((tc + 8, W), F32), pltpu.VMEM((1, W), F32)],
        compiler_params=_cparams("parallel", "arbitrary"),
    )(u3, u3, x3, cw, cb, wr, br, wi, bi, lam, wo, gpost)


def _pair_block_diag(w):
    n, d, _ = w.shape
    w = w.reshape(n // 2, 2, d, d)
    z = jnp.zeros_like(w[:, 0])
    top = jnp.concatenate([w[:, 0], z], axis=2)
    bot = jnp.concatenate([z, w[:, 1]], axis=2)
    return jnp.concatenate([top, bot], axis=1)


def _split_pair(q):
    lane = lax.broadcasted_iota(jnp.int32, q.shape, 1)
    zero = jnp.zeros_like(q)
    return jnp.where(lane < HEAD_DIM, q, zero), jnp.where(lane >= HEAD_DIM, q, zero)


def _dilated_kernel(q_ref, kp_ref, kc_ref, vp_ref, vc_ref, band_ref, o_ref, lse_ref,
                    s_scr, p_scr, *, blk):
    n = pl.program_id(1)
    band = band_ref[...]
    bias = jnp.concatenate(
        [band[:, :blk] + jnp.where(n == 0, NEG_INF, 0.0).astype(F32), band[:, blk:]], axis=1)
    lane = lax.broadcasted_iota(jnp.int32, (blk, LANES), 1)
    low = lane < HEAD_DIM
    npair = q_ref.shape[2] // LANES
    pair = lambda pr: slice(pr * LANES, (pr + 1) * LANES)

    for pr in range(npair):
        kcat = jnp.concatenate([kp_ref[0, :, pair(pr)], kc_ref[0, :, pair(pr)]], axis=0)
        for c, qh in enumerate(_split_pair(q_ref[0, :, pair(pr)])):
            s_scr[2 * pr + c] = _dot_nt(qh, kcat) + bias

    row_max = []
    for c in range(2 * npair):
        s = s_scr[c]
        m = jnp.max(s, axis=-1, keepdims=True)
        p_scr[c] = jnp.exp(s - m).astype(BF16)
        row_max.append(m)

    ones = jnp.ones((2 * blk, LANES), BF16)
    low2 = lax.broadcasted_iota(jnp.int32, (2 * blk, LANES), 1) < HEAD_DIM
    for pr in range(npair):
        vcat = jnp.concatenate([vp_ref[0, :, pair(pr)], vc_ref[0, :, pair(pr)]], axis=0)
        a0 = _dot(p_scr[2 * pr], jnp.where(low2, vcat, ones))
        a1 = _dot(p_scr[2 * pr + 1], jnp.where(low2, ones, vcat))
        num = jnp.where(low, a0, a1)
        den = pltpu.roll(jnp.where(low, a1, a0), HEAD_DIM, 1)
        m = jnp.where(low, row_max[2 * pr], row_max[2 * pr + 1])
        o_ref[0, :, pair(pr)] = (num / den).astype(o_ref.dtype)
        lse_ref[0, :, pair(pr)] = m + jnp.log(den)


def _dilated_band():
    blk = DIL_BLOCK
    i = np.arange(blk)[:, None]
    j = np.arange(2 * blk)[None, :]
    return jnp.asarray(np.where((j >= i) & (j <= i + blk), 0.0, -np.inf).astype(np.float32))


def _dilated_pattern(ud, band, batch):
    d, rows, N3 = ud.shape
    D = N3 // 3
    blk = DIL_BLOCK
    nb = rows // (batch * blk)
    cur = lambda b, n, r: b * nb + n
    prev = lambda b, n, r: b * nb + jnp.maximum(n - 1, 0)
    out_spec = pl.BlockSpec((1, blk, D), lambda b, n, r: (r, cur(b, n, r), 0))
    return pl.pallas_call(
        functools.partial(_dilated_kernel, blk=blk),
        grid=(batch, nb, d),
        in_specs=[
            pl.BlockSpec((1, blk, D), lambda b, n, r: (r, cur(b, n, r), 0)),
            pl.BlockSpec((1, blk, D), lambda b, n, r: (r, prev(b, n, r), 1)),
            pl.BlockSpec((1, blk, D), lambda b, n, r: (r, cur(b, n, r), 1)),
            pl.BlockSpec((1, blk, D), lambda b, n, r: (r, prev(b, n, r), 2)),
            pl.BlockSpec((1, blk, D), lambda b, n, r: (r, cur(b, n, r), 2)),
            pl.BlockSpec((blk, 2 * blk), lambda b, n, r: (0, 0)),
        ],
        out_specs=[out_spec, out_spec],
        out_shape=[jax.ShapeDtypeStruct((d, rows, D), BF16),
                   jax.ShapeDtypeStruct((d, rows, D), F32)],
        scratch_shapes=[pltpu.VMEM((D // HEAD_DIM, blk, 2 * blk), F32),
                        pltpu.VMEM((D // HEAD_DIM, blk, 2 * blk), BF16)],
        compiler_params=_cparams("parallel", "parallel", "parallel"),
    )(ud, ud, ud, ud, ud, band)


def _merge_outproj_kernel(*refs, npat):
    o_refs, l_refs = refs[:npat], refs[npat:2 * npat]
    w_ref, x_ref, g_ref, out_ref, o_nat, l_nat = refs[2 * npat:]
    tm, width = x_ref.shape
    nchunk = width // LANES
    base = lambda p, c: (p * nchunk + c) * tm
    for p in range(npat):
        d = o_refs[p].shape[0]
        for c in range(nchunk):
            sl = slice(c * LANES, (c + 1) * LANES)
            for r in range(d):
                rows = pl.ds(base(p, c) + r, tm // d, stride=d)
                o_nat[rows, :] = o_refs[p][r, :, sl].astype(F32)
                l_nat[rows, :] = l_refs[p][r, :, sl]

    def natural(ref, p):
        return jnp.concatenate([ref[base(p, c):base(p, c) + tm, :] for c in range(nchunk)], axis=1)

    lses = [natural(l_nat, p) for p in range(npat)]
    m = functools.reduce(jnp.maximum, lses)
    num = den = None
    for p in range(npat):
        wgt = jnp.exp(lses[p] - m)
        term = wgt * natural(o_nat, p)
        num = term if num is None else num + term
        den = wgt if den is None else den + wgt
    a = (num / den).astype(BF16)
    out_ref[...] = x_ref[...] + _rms(_dot(a, w_ref[...]), g_ref[...], NORM_EPS)


def _merge_outproj(os_, lses, w, x2, g, *, tm=512):
    T, D = x2.shape
    row = pl.BlockSpec((tm, D), lambda i: (i, 0))
    pat = [pl.BlockSpec((o.shape[0], tm // o.shape[0], D), lambda i: (0, i, 0)) for o in os_]
    npat = len(os_)
    return pl.pallas_call(
        functools.partial(_merge_outproj_kernel, npat=npat),
        grid=(T // tm,),
        in_specs=pat + pat + [pl.BlockSpec((D, D), lambda i: (0, 0)), row,
                              pl.BlockSpec((1, D), lambda i: (0, 0))],
        out_specs=row,
        out_shape=jax.ShapeDtypeStruct((T, D), F32),
        scratch_shapes=[pltpu.VMEM((npat * (D // LANES) * tm, LANES), F32),
                        pltpu.VMEM((npat * (D // LANES) * tm, LANES), F32)],
        compiler_params=_cparams("parallel"),
    )(*os_, *lses, w, x2, g)


def _diff_kernel(q_ref, k_ref, v_ref, lq1_ref, lk1_ref, lq2_ref, lk2_ref, g_ref, o_ref,
                 s_scr, p_scr, m_scr, alpha_scr, acc_scr, *, blk, lam_init):
    qi = pl.program_id(2)
    qs = _split_pair(q_ref[0])
    m_scr[...] = jnp.full(m_scr.shape, NEG_INF, F32)
    acc_scr[...] = jnp.zeros(acc_scr.shape, F32)
    lane_tiles = blk // LANES

    def tile(j, masked):
        start = pl.multiple_of(j * blk, blk)
        k = k_ref[0, pl.ds(start, blk), :]
        v1 = jnp.concatenate([v_ref[0, pl.ds(start, blk), :], jnp.ones((blk, LANES), BF16)], axis=1)
        for c in range(2):
            s = _dot_nt(qs[c], k)
            if masked:
                row = lax.broadcasted_iota(jnp.int32, (blk, blk), 0)
                col = lax.broadcasted_iota(jnp.int32, (blk, blk), 1)
                s = jnp.where(col <= row, s, NEG_INF)
            s_scr[c] = s
        for c in range(2):
            m_old = m_scr[c]
            row_max = jnp.max(s_scr[c], axis=-1, keepdims=True)
            m_new = jnp.maximum(m_old, jnp.broadcast_to(row_max, (blk, LANES)))
            alpha_scr[c] = jnp.exp(m_old - m_new)
            m_scr[c] = m_new
        for c in range(2):
            m = jnp.concatenate([m_scr[c]] * lane_tiles, axis=1)
            p_scr[c] = jnp.exp(s_scr[c] - m).astype(BF16)
        for c in range(2):
            alpha = jnp.concatenate([alpha_scr[c]] * 2, axis=1)
            acc_scr[c] = alpha * acc_scr[c] + _dot(p_scr[c], v1)

    tile(qi, True)

    def body(j, carry):
        tile(j, False)
        return carry

    lax.fori_loop(0, qi, body, 0)

    lam = (jnp.exp(jnp.sum(lq1_ref[...] * lk1_ref[...], axis=-1, keepdims=True))
           - jnp.exp(jnp.sum(lq2_ref[...] * lk2_ref[...], axis=-1, keepdims=True)) + lam_init)
    a0, a1 = acc_scr[0], acc_scr[1]
    o = a0[:, :LANES] / a0[:, LANES:] - lam * (a1[:, :LANES] / a1[:, LANES:])
    o_ref[0] = (_rms(o, g_ref[...], SUBLN_EPS) * (1.0 - lam_init)).astype(o_ref.dtype)


def _diff_attention(u3, lq1, lk1, lq2, lk2, subln, lam_init, *, blk=512):
    B, S, N3 = u3.shape
    D = N3 // 3
    npair = D // LANES
    vec = lambda n: pl.BlockSpec((1, n), lambda b, h, i: (0, 0))
    return pl.pallas_call(
        functools.partial(_diff_kernel, blk=blk, lam_init=lam_init),
        grid=(B, npair, S // blk),
        in_specs=[
            pl.BlockSpec((1, blk, LANES), lambda b, h, i: (b, i, h)),
            pl.BlockSpec((1, S, LANES), lambda b, h, i: (b, 0, npair + h)),
            pl.BlockSpec((1, S, LANES), lambda b, h, i: (b, 0, 2 * npair + h)),
            vec(HEAD_DIM), vec(HEAD_DIM), vec(HEAD_DIM), vec(HEAD_DIM), vec(LANES),
        ],
        out_specs=pl.BlockSpec((1, blk, LANES), lambda b, h, i: (b, i, h)),
        out_shape=jax.ShapeDtypeStruct((B, S, D), BF16),
        scratch_shapes=[pltpu.VMEM((2, blk, blk), F32), pltpu.VMEM((2, blk, blk), BF16),
                        pltpu.VMEM((2, blk, LANES), F32), pltpu.VMEM((2, blk, LANES), F32),
                        pltpu.VMEM((2, blk, 2 * LANES), F32)],
        compiler_params=_cparams("parallel", "parallel", "arbitrary"),
    )(u3, u3, u3, lq1, lk1, lq2, lk2, subln)


def _stick_kernel(q_ref, k_ref, v_ref, o_ref, z_scr, hi_scr, cs_scr, att_scr,
                  run_scr, acc_scr, *, blk, sub):
    qi = pl.program_id(2)
    qs = _split_pair(q_ref[0])
    nsub = blk // sub
    r2 = lax.broadcasted_iota(jnp.int32, (sub, sub), 0)
    c2 = lax.broadcasted_iota(jnp.int32, (sub, sub), 1)
    later = jnp.where(c2 < r2, 1.0, 0.0).astype(BF16)
    run_scr[...] = jnp.zeros(run_scr.shape, F32)
    acc_scr[...] = jnp.zeros(acc_scr.shape, F32)
    sign_bit = jnp.int32(-2 ** 31)

    def tile(j, masked):
        start = pl.multiple_of(j * blk, blk)
        k = k_ref[0, pl.ds(start, blk), :]
        v = v_ref[0, pl.ds(start, blk), :]
        if masked:
            row = lax.broadcasted_iota(jnp.int32, (blk, blk), 0)
            col = lax.broadcasted_iota(jnp.int32, (blk, blk), 1)
            strict = col < row

        def softplus_stage(h):
            off = run_scr[h]
            for s in reversed(range(nsub)):
                sl = slice(s * sub, (s + 1) * sub)
                z = z_scr[h, :, sl]
                neg_abs = lax.bitcast_convert_type(
                    lax.bitcast_convert_type(z, jnp.int32) | sign_bit, F32)
                sp = jnp.maximum(z, 0.0) + jnp.log2(1.0 + jnp.exp2(neg_abs))
                if masked:
                    sp = jnp.where(strict[:, sl], sp, 0.0)
                hi_scr[h, :, sl] = sp.astype(BF16)
                z_scr[h, :, sl] = z - sp - jnp.concatenate([off] * (sub // LANES), axis=1)
                rowsum = jnp.sum(sp, axis=-1, keepdims=True)
                off = off + jnp.broadcast_to(rowsum, off.shape)
            run_scr[h] = off

        def suffix_stage(h):
            for s in range(nsub):
                sl = slice(s * sub, (s + 1) * sub)
                cs_scr[h, :, sl] = _dot(hi_scr[h, :, sl], later)

        def weight_stage(h):
            att = jnp.exp2(z_scr[h] - cs_scr[h])
            if masked:
                att = jnp.where(strict, att, 0.0)
            att_scr[h] = att.astype(BF16)
            acc_scr[h] += _dot(att_scr[h], v)

        for h in range(2):
            z_scr[h] = _dot_nt(qs[h], k)
        softplus_stage(0)
        suffix_stage(0)
        softplus_stage(1)
        suffix_stage(1)
        weight_stage(0)
        weight_stage(1)

    tile(qi, True)

    def body(jj, carry):
        tile(qi - 1 - jj, False)
        return carry

    lax.fori_loop(0, qi, body, 0)
    lane = lax.broadcasted_iota(jnp.int32, (blk, LANES), 1)
    o_ref[0] = jnp.where(lane < HEAD_DIM, acc_scr[0], acc_scr[1]).astype(o_ref.dtype)


def _stick_attention(u3, *, blk=512, sub=256):
    B, S, N3 = u3.shape
    D = N3 // 3
    npair = D // LANES
    return pl.pallas_call(
        functools.partial(_stick_kernel, blk=blk, sub=sub),
        grid=(B, npair, S // blk),
        in_specs=[
            pl.BlockSpec((1, blk, LANES), lambda b, h, i: (b, i, h)),
            pl.BlockSpec((1, S, LANES), lambda b, h, i: (b, 0, npair + h)),
            pl.BlockSpec((1, S, LANES), lambda b, h, i: (b, 0, 2 * npair + h)),
        ],
        out_specs=pl.BlockSpec((1, blk, LANES), lambda b, h, i: (b, i, h)),
        out_shape=jax.ShapeDtypeStruct((B, S, D), BF16),
        scratch_shapes=[pltpu.VMEM((2, blk, blk), F32), pltpu.VMEM((2, blk, blk), BF16),
                        pltpu.VMEM((2, blk, blk), F32), pltpu.VMEM((2, blk, blk), BF16),
                        pltpu.VMEM((2, blk, LANES), F32), pltpu.VMEM((2, blk, LANES), F32)],
        compiler_params=_cparams("parallel", "parallel", "arbitrary"),
    )(u3, u3, u3)


def _rope_tables(positions):
    half = HEAD_DIM // 2
    inv = ROPE_THETA ** (-jnp.arange(0, HEAD_DIM, 2, dtype=F32) / HEAD_DIM)
    ang = positions.astype(F32).reshape(-1, 1) * inv
    cos = jnp.cos(ang)
    sin = jnp.sin(ang)
    reps = LANES // HEAD_DIM
    cos_t = jnp.tile(jnp.concatenate([cos, cos], axis=1), (1, reps))
    sin_t = jnp.tile(jnp.concatenate([-sin, sin], axis=1), (1, reps))
    return cos_t, sin_t


def kernel(x, p, positions, ln_mix_pre, ln_mix_post, ln_ffn_pre, ln_ffn_post, ln_ple,
           w_ffn_in, w_ffn_out, w_ple_gate, b_ple_gate, w_ple_proj,
           a_w_in, a_conv_w, a_conv_b, a_gate_r_w, a_gate_r_b, a_gate_i_w, a_gate_i_b,
           a_lambda, a_w_out, b_w_qkv, b_w_out, c_w_qkv, c_w_out,
           d_w_qkv, d_lambda_q1, d_lambda_k1, d_lambda_q2, d_lambda_k2, d_subln, d_w_out):
    B, S, D = x.shape
    depth = p.shape[0]
    T = B * S
    cos_t, sin_t = _rope_tables(positions)
    assert all(w == d * DIL_BLOCK and S % w == 0 for w, d in DIL_PATTERNS)
    band = _dilated_band()
    bf = lambda w: w.astype(BF16)
    vec = lambda v: v.reshape(1, -1)

    x2 = x.reshape(T, D)
    for i in range(depth):
        kind, j = i % 4, i // 4
        gpre, gpost = vec(ln_mix_pre[i]), vec(ln_mix_post[i])
        if kind == 0:
            u = _norm_proj(x2, gpre, bf(a_w_in[j]), cos_t, sin_t,
                           rope=False, qscale=None, out_dtype=F32)
            x2 = _rglru_core(
                u.reshape(B, S, -1), x2.reshape(B, S, D), a_conv_w[j], vec(a_conv_b[j]),
                bf(_pair_block_diag(a_gate_r_w[j])), vec(a_gate_r_b[j]),
                bf(_pair_block_diag(a_gate_i_w[j])), vec(a_gate_i_b[j]),
                vec(a_lambda[j]), bf(a_w_out[j]), gpost).reshape(T, D)
        elif kind == 1:
            dils = [d for _, d in DIL_PATTERNS]
            us = _norm_proj(x2, gpre, bf(b_w_qkv[j]), cos_t, sin_t, rope=True, qscale=QK_SCALE,
                            out_dtype=BF16, strides=[d for d in dils if d > 1])
            by_stride = dict(zip([1] + [d for d in dils if d > 1], [us[0][None]] + list(us[1:])))
            parts = [_dilated_pattern(by_stride[d], band, B) for d in dils]
            x2 = _merge_outproj([o for o, _ in parts], [l for _, l in parts],
                                bf(b_w_out[j]), x2, gpost)
        else:
            if kind == 2:
                u = _norm_proj(x2, gpre, bf(c_w_qkv[j]), cos_t, sin_t,
                               rope=False, qscale=QK_SCALE * LOG2_E, out_dtype=BF16)
                o = _stick_attention(u.reshape(B, S, -1))
                w_o = c_w_out[j]
            else:
                u = _norm_proj(x2, gpre, bf(d_w_qkv[j]), cos_t, sin_t,
                               rope=True, qscale=QK_SCALE, out_dtype=BF16)
                lam_init = 0.8 - 0.6 * math.exp(-0.3 * i)
                o = _diff_attention(u.reshape(B, S, -1), vec(d_lambda_q1[j]), vec(d_lambda_k1[j]),
                                    vec(d_lambda_q2[j]), vec(d_lambda_k2[j]), vec(d_subln[j]),
                                    lam_init)
                w_o = d_w_out[j]
            x2 = _outproj(o.reshape(T, D), bf(w_o), x2, gpost)
        x2 = _ffn_ple(x2, vec(ln_ffn_pre[i]), bf(w_ffn_in[i]), bf(w_ffn_out[i]),
                      vec(ln_ffn_post[i]), p[i].reshape(T, -1), bf(w_ple_gate[i]),
                      vec(b_ple_gate[i]), bf(w_ple_proj[i]), vec(ln_ple[i]))
    return x2.reshape(B, S, D)
```
